```python
import math
import jax, jax.numpy as jnp
from jax import lax
import numpy as np

D_MODEL = 1024
BATCH = 16
SEQ = 2048
DEPTH = 1
DEC_BATCH = 128
DEC_SEQ = 8
PAST_LEN = 8192
PAGE_SIZE = 128

N_META = 16
N_HEADS = 8
QK_NOPE = 64
QK_ROPE = 32
QK_HEAD = QK_NOPE + QK_ROPE
V_HEAD = D_MODEL // N_HEADS
Q_LORA = 3 * D_MODEL // 8
KV_LORA = D_MODEL // 4
POOL_WIDTH = D_MODEL // 2
POOL_WINDOWS = (2, 4, 8, 16)
POOL_GROUPS = len(POOL_WINDOWS)
POOL_GROUP_WIDTH = POOL_WIDTH // POOL_GROUPS
POOL_GROUP_OUT = D_MODEL // POOL_GROUPS
POOL_STATE = max(POOL_WINDOWS) - 1
D_FF = 4 * D_MODEL
ROPE_BASE = 10000.0
EPS = 1e-6
Q_BLOCK = 128
SCALE = QK_HEAD ** -0.5
IN_WIDTHS = (POOL_WIDTH, Q_LORA, KV_LORA, QK_ROPE, 2 * D_MODEL)
D_IN = sum(IN_WIDTHS)
IN_SPLITS = tuple(int(s) for s in np.cumsum(IN_WIDTHS)[:-1])

kernel_name = "hybrid_pool_mla_decoder_step"


def rms_norm(x, g):
    xf = x.astype(jnp.float32)
    y = xf * lax.rsqrt(jnp.mean(xf * xf, axis=-1, keepdims=True) + EPS)
    return (y * g.astype(jnp.float32)).astype(x.dtype)


def head_gain(g_nope, g_rope_pair):
    return jnp.concatenate([g_nope, g_rope_pair, g_rope_pair])


def rope_tables(pos):
    inv = ROPE_BASE ** (-jnp.arange(0, QK_ROPE, 2, dtype=jnp.float32) / QK_ROPE)
    ang = pos.astype(jnp.float32)[:, None] * inv[None, :]
    cos = jnp.concatenate([jnp.cos(ang), jnp.cos(ang)], axis=-1)
    sin = jnp.concatenate([jnp.sin(ang), jnp.sin(ang)], axis=-1)
    return cos, sin


def apply_rope(x, cos, sin):
    xf = x.astype(jnp.float32)
    x1, x2 = xf[..., :QK_ROPE // 2], xf[..., QK_ROPE // 2:]
    rot = jnp.concatenate([-x2, x1], axis=-1)
    return (xf * cos + rot * sin).astype(x.dtype)


def mixer_inputs(xn, pos, w_in, g_q_lat, w_uq, g_kv_lat, g_qn_nope, g_qn_rope):
    proj = xn @ w_in
    u_pool, q_lat, c_kv, k_r, gate_logits = jnp.split(proj, IN_SPLITS, axis=-1)
    cos, sin = rope_tables(pos)
    q = rms_norm(q_lat, g_q_lat) @ w_uq
    q = q.reshape(*xn.shape[:-1], N_HEADS, QK_HEAD)
    q = jnp.concatenate([q[..., :QK_NOPE], apply_rope(q[..., QK_NOPE:], cos[:, None, :], sin[:, None, :])], axis=-1)
    q = rms_norm(q, head_gain(g_qn_nope, g_qn_rope))
    c_kv = rms_norm(c_kv, g_kv_lat)
    k_r = apply_rope(k_r, cos, sin)
    return u_pool, q, c_kv, k_r, gate_logits


def mla_keys(c_kv, k_r, w_uk, g_kn_nope, g_kn_rope):
    k_nope = jnp.einsum('...tr,rhd->...thd', c_kv, w_uk)
    k_rope = jnp.broadcast_to(k_r[..., None, :], k_nope.shape[:-1] + (QK_ROPE,))
    k = jnp.concatenate([k_nope, k_rope], axis=-1)
    return rms_norm(k, head_gain(g_kn_nope, g_kn_rope))


def prompt_attention(q, c_kv, k_r, w_uk, w_uv, g_kn_nope, g_kn_rope):
    B, T = q.shape[0], q.shape[1]
    k = mla_keys(c_kv, k_r, w_uk, g_kn_nope, g_kn_rope)
    v = jnp.einsum('btr,rhv->bthv', c_kv, w_uv)
    n_blk = -(-T // Q_BLOCK)
    pad = n_blk * Q_BLOCK - T
    qb = jnp.pad(q, ((0, 0), (0, pad), (0, 0), (0, 0)))
    qb = qb.reshape(B, n_blk, Q_BLOCK, N_HEADS, QK_HEAD).transpose(1, 0, 2, 3, 4)
    k_pos = jnp.arange(T)

    def block(args):
        qi, blk = args
        q_pos = blk * Q_BLOCK + jnp.arange(Q_BLOCK)
        s = jnp.einsum('bqhd,bkhd->bhqk', qi, k, preferred_element_type=jnp.float32) * SCALE
        s = jnp.where(k_pos[None, :] <= q_pos[:, None], s, -jnp.inf)
        p = jax.nn.softmax(s, axis=-1).astype(v.dtype)
        return jnp.einsum('bhqk,bkhv->bqhv', p, v)

    o = lax.map(block, (qb, jnp.arange(n_blk)))
    o = o.transpose(1, 0, 2, 3, 4).reshape(B, n_blk * Q_BLOCK, N_HEADS * V_HEAD)
    return o[:, :T]


def sample_attention(q, c_new, kr_new, cache_kv_latent, cache_k_rope, page_table,
                     w_uk, w_uv, g_kn_nope, g_kn_rope):
    past = page_table.shape[1] * PAGE_SIZE
    S = q.shape[1]
    k_pos = jnp.arange(past + S)
    q_pos = past + jnp.arange(S)
    mask = k_pos[None, :] <= q_pos[:, None]

    def one_seq(args):
        pt, qi, ci, kri = args
        c_all = jnp.concatenate([cache_kv_latent[pt].reshape(past, KV_LORA).astype(ci.dtype), ci], axis=0)
        kr_all = jnp.concatenate([cache_k_rope[pt].reshape(past, QK_ROPE).astype(kri.dtype), kri], axis=0)
        k = mla_keys(c_all, kr_all, w_uk, g_kn_nope, g_kn_rope)
        s = jnp.einsum('qhd,khd->hqk', qi, k, preferred_element_type=jnp.float32) * SCALE
        s = jnp.where(mask[None], s, -jnp.inf)
        p = jax.nn.softmax(s, axis=-1).astype(c_all.dtype)
        ctx = jnp.einsum('hqk,kr->qhr', p, c_all)
        return jnp.einsum('qhr,rhv->qhv', ctx, w_uv).reshape(S, N_HEADS * V_HEAD)

    return lax.map(one_seq, (page_table, q, c_new, kr_new))


def pool_mixer(u, prefix, pos, w_pool_map, pool_scale):
    B, T = u.shape[0], u.shape[1]
    full = jnp.concatenate([prefix, u], axis=1).astype(jnp.float32)
    cs = jnp.concatenate([jnp.zeros((B, 1, POOL_WIDTH), jnp.float32), jnp.cumsum(full, axis=1)], axis=1)
    start = POOL_STATE + 1
    means = []
    for g, w in enumerate(POOL_WINDOWS):
        c0, c1 = g * POOL_GROUP_WIDTH, (g + 1) * POOL_GROUP_WIDTH
        win = cs[:, start:start + T, c0:c1] - cs[:, start - w:start - w + T, c0:c1]
        cnt = jnp.minimum(w, pos + 1).astype(jnp.float32)[None, :, None]
        means.append(win / cnt)
    d = (jnp.concatenate(means, axis=-1) - u.astype(jnp.float32)).astype(u.dtype)
    d = d.reshape(B, T, POOL_GROUPS, POOL_GROUP_WIDTH)
    out = jnp.einsum('btgc,gcd->btgd', d, w_pool_map).reshape(B, T, D_MODEL)
    return out * pool_scale


def merge_and_mlp(x, pool_out, attn_out, gate_logits, w_o, g_mlp, w_ff1, w_ff2):
    g = jax.nn.sigmoid(gate_logits.astype(jnp.float32)).astype(x.dtype)
    g_pool, g_att = g[..., :D_MODEL], g[..., D_MODEL:]
    h = x + (g_pool * pool_out + g_att * attn_out) @ w_o
    z = jax.nn.relu(rms_norm(h, g_mlp) @ w_ff1)
    return h + (z * z) @ w_ff2


def setup_inputs(seed: int = 0) -> dict:
    key = jax.random.key(seed)
    ks = jax.random.split(key, 26)
    n_pages = PAST_LEN // PAGE_SIZE
    n_used = DEC_BATCH * n_pages
    n_phys = n_used + n_used // 4
    f32 = jnp.float32

    def nrm(k, shape, scale):
        return jax.random.normal(k, shape, f32) * scale

    def gain(k, n):
        return 1.0 + 0.1 * jax.random.normal(k, (n,), f32)

    page_table = jax.random.permutation(ks[5], n_phys)[:n_used].reshape(DEC_BATCH, n_pages).astype(jnp.int32)
    return {
        "x_prompt": nrm(ks[0], (BATCH, SEQ, D_MODEL), 1.0),
        "x_sample": nrm(ks[1], (DEC_BATCH, DEC_SEQ, D_MODEL), 1.0),
        "cache_kv_latent": nrm(ks[2], (n_phys, PAGE_SIZE, KV_LORA), 1.0),
        "cache_k_rope": nrm(ks[3], (n_phys, PAGE_SIZE, QK_ROPE), 1.0),
        "state_pool": nrm(ks[4], (DEC_BATCH, POOL_STATE, POOL_WIDTH), 1.0),
        "page_table": page_table,
        "meta_tokens": nrm(ks[6], (N_META, D_MODEL), 1.0),
        "g_attn": gain(ks[7], D_MODEL),
        "w_in": nrm(ks[8], (D_MODEL, D_IN), D_MODEL ** -0.5),
        "g_q_lat": gain(ks[9], Q_LORA),
        "w_uq": nrm(ks[10], (Q_LORA, N_HEADS * QK_HEAD), Q_LORA ** -0.5),
        "g_kv_lat": gain(ks[11], KV_LORA),
        "g_qn_nope": gain(ks[12], QK_NOPE),
        "g_qn_rope": gain(ks[13], QK_ROPE // 2),
        "w_uk": nrm(ks[14], (KV_LORA, N_HEADS, QK_NOPE), KV_LORA ** -0.5),
        "g_kn_nope": gain(ks[15], QK_NOPE),
        "g_kn_rope": gain(ks[16], QK_ROPE // 2),
        "w_uv": nrm(ks[17], (KV_LORA, N_HEADS, V_HEAD), KV_LORA ** -0.5),
        "w_pool_map": nrm(ks[18], (POOL_GROUPS, POOL_GROUP_WIDTH, POOL_GROUP_OUT), POOL_GROUP_WIDTH ** -0.5),
        "pool_scale": gain(ks[19], D_MODEL),
        "w_o": nrm(ks[20], (D_MODEL, D_MODEL), D_MODEL ** -0.5),
        "g_mlp": gain(ks[21], D_MODEL),
        "w_ff1": nrm(ks[22], (D_MODEL, D_FF), D_MODEL ** -0.5),
        "w_ff2": nrm(ks[23], (D_FF, D_MODEL), D_FF ** -0.5),
    }


def reference(x_prompt, x_sample, cache_kv_latent, cache_k_rope, state_pool, page_table,
              meta_tokens, g_attn, w_in, g_q_lat, w_uq, g_kv_lat, g_qn_nope, g_qn_rope,
              w_uk, g_kn_nope, g_kn_rope, w_uv, w_pool_map, pool_scale, w_o, g_mlp, w_ff1, w_ff2):
    B = x_prompt.shape[0]
    meta = jnp.broadcast_to(meta_tokens.astype(x_prompt.dtype)[None], (B, N_META, D_MODEL))
    xp = jnp.concatenate([meta, x_prompt], axis=1)
    T = xp.shape[1]
    pos_p = jnp.arange(T)
    xn_p = rms_norm(xp, g_attn)
    u_p, q_p, c_p, kr_p, gl_p = mixer_inputs(xn_p, pos_p, w_in, g_q_lat, w_uq, g_kv_lat, g_qn_nope, g_qn_rope)
    pool_p = pool_mixer(u_p, jnp.zeros((B, POOL_STATE, POOL_WIDTH), u_p.dtype), pos_p, w_pool_map, pool_scale)
    att_p = prompt_attention(q_p, c_p, kr_p, w_uk, w_uv, g_kn_nope, g_kn_rope)
    y_full = merge_and_mlp(xp, pool_p, att_p, gl_p, w_o, g_mlp, w_ff1, w_ff2)
    y_prompt = y_full[:, N_META:]
    pool_state_prompt = u_p[:, T - POOL_STATE:]

    past = page_table.shape[1] * PAGE_SIZE
    S = x_sample.shape[1]
    pos_s = past + jnp.arange(S)
    xn_s = rms_norm(x_sample, g_attn)
    u_s, q_s, c_s, kr_s, gl_s = mixer_inputs(xn_s, pos_s, w_in, g_q_lat, w_uq, g_kv_lat, g_qn_nope, g_qn_rope)
    prefix = state_pool.astype(u_s.dtype)
    pool_s = pool_mixer(u_s, prefix, pos_s, w_pool_map, pool_scale)
    att_s = sample_attention(q_s, c_s, kr_s, cache_kv_latent, cache_k_rope, page_table,
                             w_uk, w_uv, g_kn_nope, g_kn_rope)
    y_sample = merge_and_mlp(x_sample, pool_s, att_s, gl_s, w_o, g_mlp, w_ff1, w_ff2)
    pool_state_sample = jnp.concatenate([prefix, u_s], axis=1)[:, -POOL_STATE:]

    return (y_prompt, y_sample, c_p, kr_p, pool_state_prompt, c_s, kr_s, pool_state_sample)
```

```python
import functools

import jax
import jax.numpy as jnp
from jax import lax
from jax.experimental import pallas as pl
from jax.experimental.pallas import tpu as pltpu

D_MODEL = 1024
N_META = 16
N_HEADS = 8
QK_NOPE = 64
QK_ROPE = 32
ROPE_HALF = QK_ROPE // 2
QK_HEAD = QK_NOPE + QK_ROPE
V_HEAD = D_MODEL // N_HEADS
Q_LORA = 3 * D_MODEL // 8
KV_LORA = D_MODEL // 4
POOL_WIDTH = D_MODEL // 2
POOL_WINDOWS = (2, 4, 8, 16)
POOL_GROUPS = len(POOL_WINDOWS)
POOL_GROUP_WIDTH = POOL_WIDTH // POOL_GROUPS
POOL_GROUP_OUT = D_MODEL // POOL_GROUPS
POOL_STATE = max(POOL_WINDOWS) - 1
D_FF = 4 * D_MODEL
ROPE_BASE = 10000.0
EPS = 1e-6
SCALE = QK_HEAD ** -0.5
PAGE_SIZE = 128

LANES = 128
V7X_VMEM_LIMIT_BYTES = 56 * 2 ** 20

HEAD_PAD = LANES
QK_PAD = N_HEADS * HEAD_PAD
KR_PAD = LANES
IN_U0, IN_Q0, IN_C0, IN_KR0 = 0, POOL_WIDTH, POOL_WIDTH + Q_LORA, POOL_WIDTH + Q_LORA + KV_LORA
IN_G0 = IN_KR0 + KR_PAD
IN_PAD = IN_G0 + 2 * D_MODEL
N_TABS = 6

F32 = jnp.float32
BF16 = jnp.bfloat16


def _dot(a, b):
    return jnp.dot(a, b, preferred_element_type=F32)


def _dot_nt(a, b):
    return lax.dot_general(a, b, (((1,), (1,)), ((), ())), preferred_element_type=F32)


def _rms(x, g):
    return x * lax.rsqrt(jnp.mean(x * x, axis=-1, keepdims=True) + EPS) * g


def _const_spec(shape):
    zeros = (0,) * len(shape)
    return pl.BlockSpec(shape, lambda *_: zeros, pipeline_mode=pl.Buffered(1))


def _params(n_axes):
    return pltpu.CompilerParams(dimension_semantics=("arbitrary",) * n_axes,
                                vmem_limit_bytes=V7X_VMEM_LIMIT_BYTES)


def _rope_tables(pos):
    p = pos.shape[0]
    inv = ROPE_BASE ** (-jnp.arange(0, QK_ROPE, 2, dtype=F32) / QK_ROPE)
    ang = pos.astype(F32)[:, None] * inv[None, :]
    cos, sin = jnp.cos(ang), jnp.sin(ang)
    z = lambda n: jnp.zeros((p, n), F32)
    q_cos = jnp.concatenate([jnp.ones((p, QK_NOPE), F32), cos, cos, z(HEAD_PAD - QK_HEAD)], axis=1)
    q_up = jnp.concatenate([z(QK_NOPE + ROPE_HALF), sin, z(HEAD_PAD - QK_HEAD)], axis=1)
    q_dn = jnp.concatenate([z(QK_NOPE), -sin, z(HEAD_PAD - QK_NOPE - ROPE_HALF)], axis=1)
    r_cos = jnp.concatenate([cos, cos, z(KR_PAD - QK_ROPE)], axis=1)
    r_up = jnp.concatenate([z(ROPE_HALF), sin, z(KR_PAD - QK_ROPE)], axis=1)
    r_dn = jnp.concatenate([-sin, z(KR_PAD - ROPE_HALF)], axis=1)
    return jnp.concatenate([q_cos, q_up, q_dn, r_cos, r_up, r_dn], axis=1)


def _head_gain(g_nope, g_rope):
    g = jnp.concatenate([g_nope, g_rope, g_rope, jnp.zeros((HEAD_PAD - QK_HEAD,), F32)])
    return g.reshape(1, HEAD_PAD).astype(F32)


def _prep_weights(w_in, w_uq, w_uk, w_uv, w_pool_map, w_o, w_ff1, w_ff2):
    w_in_pad = jnp.concatenate(
        [w_in[:, :IN_KR0 + QK_ROPE], jnp.zeros((D_MODEL, KR_PAD - QK_ROPE), w_in.dtype), w_in[:, IN_KR0 + QK_ROPE:]],
        axis=1).astype(BF16)
    uq = w_uq.reshape(Q_LORA, N_HEADS, QK_HEAD)
    uq = jnp.pad(uq, ((0, 0), (0, 0), (0, HEAD_PAD - QK_HEAD))).reshape(Q_LORA, QK_PAD).astype(BF16)
    uk = jnp.pad(w_uk, ((0, 0), (0, 0), (0, HEAD_PAD - QK_NOPE))).reshape(KV_LORA, QK_PAD).astype(BF16)
    uk_t = w_uk.reshape(KV_LORA, N_HEADS * QK_NOPE).T.astype(BF16)
    uv = w_uv.reshape(KV_LORA, N_HEADS * V_HEAD).astype(BF16)
    return dict(w_in=w_in_pad, w_uq=uq, w_uk=uk, w_uk_t=uk_t, w_uv=uv, w_pool=w_pool_map.astype(BF16),
                w_o=w_o.astype(BF16), w_ff1=w_ff1.astype(BF16), w_ff2=w_ff2.astype(BF16))


def _rope_group(x, cos, up, dn):
    return x * cos + pltpu.roll(x, ROPE_HALF, 1) * up + pltpu.roll(x, LANES - ROPE_HALF, 1) * dn


def _norm_head(x, gain):
    ssq = jnp.sum(x * x, axis=-1, keepdims=True)
    return x * lax.rsqrt(ssq * (1.0 / QK_HEAD) + EPS) * gain


def _proj_kernel(x_ref, tab_ref, g_attn_ref, w_in_ref, g_q_ref, w_uq_ref, g_kv_ref, gq_ref,
                 w_uk_ref, gk_ref, w_uv_ref, u_ref, q_ref, c_ref, kr_ref, gate_ref, *kv_refs, with_kv):
    xn = _rms(x_ref[...], g_attn_ref[...]).astype(BF16)
    u_ref[...] = _dot(xn, w_in_ref[:, IN_U0:IN_Q0])

    gate = _dot(xn, w_in_ref[:, IN_G0:IN_PAD])
    gate_ref[...] = jax.nn.sigmoid(gate).astype(gate_ref.dtype)

    c = _rms(_dot(xn, w_in_ref[:, IN_C0:IN_KR0]), g_kv_ref[...])
    c_ref[...] = c

    tab = lambda t: tab_ref[:, t * LANES:(t + 1) * LANES]
    kr = _dot(xn, w_in_ref[:, IN_KR0:IN_G0])
    kr = _rope_group(kr, tab(3), tab(4), tab(5))
    kr_ref[...] = kr[:, :QK_ROPE]

    q_lat = _rms(_dot(xn, w_in_ref[:, IN_Q0:IN_C0]), g_q_ref[...]).astype(BF16)
    q_full = _dot(q_lat, w_uq_ref[...])
    q_gain = gq_ref[...]
    for h in range(N_HEADS):
        sl = slice(h * HEAD_PAD, (h + 1) * HEAD_PAD)
        qh = _rope_group(q_full[:, sl], tab(0), tab(1), tab(2))
        q_ref[:, sl] = _norm_head(qh, q_gain).astype(q_ref.dtype)

    if with_kv:
        k_ref, v_ref = kv_refs
        cb = c.astype(BF16)
        k_full = _dot(cb, w_uk_ref[...])
        kr_lanes = pltpu.roll(kr, QK_NOPE, 1)
        k_gain = gk_ref[...]
        for h in range(N_HEADS):
            sl = slice(h * HEAD_PAD, (h + 1) * HEAD_PAD)
            k_ref[:, sl] = _norm_head(k_full[:, sl] + kr_lanes, k_gain).astype(k_ref.dtype)
        v_ref[...] = _dot(cb, w_uv_ref[...]).astype(v_ref.dtype)


def _run_proj(x2d, tabs, wts, gains, *, tm, with_kv, q_dtype):
    rows = x2d.shape[0]
    assert rows % tm == 0 and tabs.shape[0] % tm == 0
    tab_blocks = tabs.shape[0] // tm
    row_spec = lambda w: pl.BlockSpec((tm, w), lambda i: (i, 0))
    in_specs = [
        row_spec(D_MODEL),
        pl.BlockSpec((tm, N_TABS * LANES), lambda i: (i % tab_blocks, 0)),
        _const_spec((1, D_MODEL)), _const_spec((D_MODEL, IN_PAD)),
        _const_spec((1, Q_LORA)), _const_spec((Q_LORA, QK_PAD)),
        _const_spec((1, KV_LORA)), _const_spec((1, HEAD_PAD)),
        _const_spec((KV_LORA, QK_PAD)), _const_spec((1, HEAD_PAD)), _const_spec((KV_LORA, QK_PAD)),
    ]
    out_shape = [
        jax.ShapeDtypeStruct((rows, POOL_WIDTH), F32),
        jax.ShapeDtypeStruct((rows, QK_PAD), q_dtype),
        jax.ShapeDtypeStruct((rows, KV_LORA), F32),
        jax.ShapeDtypeStruct((rows, QK_ROPE), F32),
        jax.ShapeDtypeStruct((rows, 2 * D_MODEL), BF16),
    ]
    out_specs = [row_spec(POOL_WIDTH), row_spec(QK_PAD), row_spec(KV_LORA), row_spec(QK_ROPE),
                 row_spec(2 * D_MODEL)]
    if with_kv:
        out_shape += [jax.ShapeDtypeStruct((rows, QK_PAD), BF16)] * 2
        out_specs += [row_spec(QK_PAD)] * 2
    return pl.pallas_call(
        functools.partial(_proj_kernel, with_kv=with_kv),
        grid=(rows // tm,), in_specs=in_specs, out_specs=out_specs, out_shape=out_shape,
        compiler_params=_params(1), name="proj",
    )(x2d, tabs, gains["g_attn"], wts["w_in"], gains["g_q_lat"], wts["w_uq"], gains["g_kv_lat"],
      gains["gq"], wts["w_uk"], gains["gk"], wts["w_uv"])


def _softmax_step(s, v, m, l, acc):
    m_new = jnp.maximum(m, jnp.max(s, axis=-1, keepdims=True))
    alpha = jnp.exp(m - m_new)
    p = jnp.exp(s - m_new)
    l = alpha * l + jnp.sum(p, axis=-1, keepdims=True)
    acc = alpha * acc + _dot(p.astype(BF16), v)
    return m_new, l, acc


def _flash_kernel(q_ref, k_ref, v_ref, km_ref, vm_ref, o_ref, *, tq):
    i = pl.program_id(2)
    q = q_ref[...]
    s = _dot_nt(q, km_ref[...])
    m = jnp.max(s, axis=-1, keepdims=True)
    p = jnp.exp(s - m)
    l = jnp.sum(p, axis=-1, keepdims=True)
    acc = _dot(p.astype(BF16), vm_ref[...])

    def body(j, carry):
        rows = pl.ds(pl.multiple_of(j * tq, tq), tq)
        return _softmax_step(_dot_nt(q, k_ref[rows, :]), v_ref[rows, :], *carry)

    m, l, acc = lax.fori_loop(0, i, body, (m, l, acc))

    rows = pl.ds(pl.multiple_of(i * tq, tq), tq)
    s = _dot_nt(q, k_ref[rows, :])
    causal = lax.broadcasted_iota(jnp.int32, s.shape, 1) <= lax.broadcasted_iota(jnp.int32, s.shape, 0)
    m, l, acc = _softmax_step(jnp.where(causal, s, -jnp.inf), v_ref[rows, :], m, l, acc)
    o_ref[...] = (acc / l).astype(o_ref.dtype)


def _run_flash(q, k, v, k_meta, v_meta, *, tq):
    b, seq, _ = q.shape
    assert seq % tq == 0
    q_spec = pl.BlockSpec((None, tq, HEAD_PAD), lambda bi, h, i: (bi, i, h))
    kv_spec = pl.BlockSpec((None, seq, HEAD_PAD), lambda bi, h, i: (bi, 0, h))
    meta_spec = pl.BlockSpec((N_META, HEAD_PAD), lambda bi, h, i: (0, h))
    return pl.pallas_call(
        functools.partial(_flash_kernel, tq=tq),
        grid=(b, N_HEADS, seq // tq),
        in_specs=[q_spec, kv_spec, kv_spec, meta_spec, meta_spec],
        out_specs=q_spec, out_shape=jax.ShapeDtypeStruct((b, seq, N_HEADS * V_HEAD), BF16),
        compiler_params=_params(3), name="flash",
    )(q, k, v, k_meta, v_meta)


def _paged_kernel(pt_ref, q_ref, cn_ref, krn_ref, w_uk_t_ref, w_uv_ref, gk_ref, cache_c_ref, cache_kr_ref,
                  o_ref, c_buf, kr_buf, sem, a_buf, qr_buf, cnew_buf, krnew_buf, *, n_pages, n_seq, s_new, tk):
    b = pl.program_id(0)
    slot = b % 2
    rows_hq = N_HEADS * s_new
    nope_rows = N_HEADS * QK_NOPE

    def page_copies(seq, sl, j):
        page = pt_ref[seq, j]
        dst = pl.ds(pl.multiple_of(j * PAGE_SIZE, PAGE_SIZE), PAGE_SIZE)
        return (pltpu.make_async_copy(cache_c_ref.at[page], c_buf.at[sl, dst, :], sem.at[0, sl]),
                pltpu.make_async_copy(cache_kr_ref.at[page], kr_buf.at[sl, dst, :], sem.at[1, sl]))

    def start_seq(seq, sl):
        def body(j, _):
            for cp in page_copies(seq, sl, j):
                cp.start()
            return 0
        lax.fori_loop(0, n_pages, body, 0)

    def wait_seq(seq, sl):
        def body(j, _):
            for cp in page_copies(seq, sl, j):
                cp.wait()
            return 0
        lax.fori_loop(0, n_pages, body, 0)

    @pl.when(b == 0)
    def _():
        start_seq(0, 0)
        a_buf[0:nope_rows, :] = w_uk_t_ref[...]
        cnew_buf[...] = jnp.zeros_like(cnew_buf)
        krnew_buf[...] = jnp.zeros_like(krnew_buf)

    @pl.when(b + 1 < n_seq)
    def _():
        start_seq(b + 1, 1 - slot)

    qg = q_ref[...] * jnp.concatenate([gk_ref[...]] * N_HEADS, axis=1)
    for h in range(N_HEADS):
        lo = h * HEAD_PAD
        q_nope = qg[:, lo:lo + QK_NOPE].astype(BF16)
        q_abs = _dot(q_nope, w_uk_t_ref[h * QK_NOPE:(h + 1) * QK_NOPE, :])
        a_buf[nope_rows + h * s_new:nope_rows + (h + 1) * s_new, :] = q_abs.astype(BF16)
        qr_buf[h * s_new:(h + 1) * s_new, :] = qg[:, lo + QK_NOPE:lo + QK_HEAD]
    cnew_buf[0:s_new, :] = cn_ref[...]
    krnew_buf[0:s_new, :] = krn_ref[...]

    wait_seq(b, slot)

    a_mat = a_buf[...]
    q_rope = qr_buf[...].astype(BF16)
    ones_rope = jnp.ones((8, QK_ROPE), BF16)

    def tile(c_t, kr_t, carry, mask):
        m, l, acc = carry
        c_bf = c_t.astype(BF16)
        r = _dot_nt(a_mat, c_bf)
        s_rope = _dot_nt(q_rope, kr_t.astype(BF16))
        ssq_rope = _dot_nt(ones_rope, (kr_t * kr_t).astype(BF16))[0:1, :]
        parts = []
        for h in range(N_HEADS):
            kn = r[h * QK_NOPE:(h + 1) * QK_NOPE, :]
            ssq = jnp.sum(kn * kn, axis=0, keepdims=True) + ssq_rope
            rinv = lax.rsqrt(ssq * (1.0 / QK_HEAD) + EPS)
            rows = slice(h * s_new, (h + 1) * s_new)
            parts.append((r[nope_rows + h * s_new:nope_rows + (h + 1) * s_new, :] + s_rope[rows, :]) * rinv)
        s = jnp.concatenate(parts, axis=0)
        if mask is not None:
            s = jnp.where(mask, s, -jnp.inf)
        return _softmax_step(s, c_bf, m, l, acc)

    carry = (jnp.full((rows_hq, 1), -jnp.inf, F32), jnp.zeros((rows_hq, 1), F32),
             jnp.zeros((rows_hq, KV_LORA), F32))

    def body(t, carry):
        keys = pl.ds(pl.multiple_of(t * tk, tk), tk)
        return tile(c_buf[slot, keys, :], kr_buf[slot, keys, :], carry, None)

    carry = lax.fori_loop(0, n_pages * PAGE_SIZE // tk, body, carry)

    shape = (rows_hq, cnew_buf.shape[0])
    tok = lax.broadcasted_iota(jnp.int32, shape, 0) % s_new
    mask = lax.broadcasted_iota(jnp.int32, shape, 1) <= tok
    m, l, acc = tile(cnew_buf[...], krnew_buf[...], carry, mask)

    ctx = (acc / l).astype(BF16)
    for h in range(N_HEADS):
        o_ref[:, h * V_HEAD:(h + 1) * V_HEAD] = _dot(ctx[h * s_new:(h + 1) * s_new, :],
                                                     w_uv_ref[:, h * V_HEAD:(h + 1) * V_HEAD])


def _run_paged(page_table, q, c_new, kr_new, wts, gains, cache_c, cache_kr, *, s_new, tk):
    n_seq, n_pages = page_table.shape
    past = n_pages * PAGE_SIZE
    assert past % tk == 0 and s_new % 8 == 0 and s_new <= PAGE_SIZE
    rows_hq = N_HEADS * s_new
    seq_spec = lambda w: pl.BlockSpec((s_new, w), lambda b, pt: (b, 0))
    const = lambda shape: pl.BlockSpec(shape, lambda b, pt: (0,) * len(shape), pipeline_mode=pl.Buffered(1))
    grid_spec = pltpu.PrefetchScalarGridSpec(
        num_scalar_prefetch=1, grid=(n_seq,),
        in_specs=[seq_spec(QK_PAD), seq_spec(KV_LORA), seq_spec(QK_ROPE),
                  const((N_HEADS * QK_NOPE, KV_LORA)), const((KV_LORA, N_HEADS * V_HEAD)), const((1, HEAD_PAD)),
                  pl.BlockSpec(memory_space=pl.ANY), pl.BlockSpec(memory_space=pl.ANY)],
        out_specs=seq_spec(N_HEADS * V_HEAD),
        scratch_shapes=[
            pltpu.VMEM((2, past, KV_LORA), F32),
            pltpu.VMEM((2, past, QK_ROPE), F32),
            pltpu.SemaphoreType.DMA((2, 2)),
            pltpu.VMEM((N_HEADS * QK_NOPE + rows_hq, KV_LORA), BF16),
            pltpu.VMEM((rows_hq, QK_ROPE), F32),
            pltpu.VMEM((PAGE_SIZE, KV_LORA), F32),
            pltpu.VMEM((PAGE_SIZE, QK_ROPE), F32),
        ])
    return pl.pallas_call(
        functools.partial(_paged_kernel, n_pages=n_pages, n_seq=n_seq, s_new=s_new, tk=tk),
        grid_spec=grid_spec, out_shape=jax.ShapeDtypeStruct((n_seq * s_new, N_HEADS * V_HEAD), F32),
        compiler_params=_params(1), name="paged",
    )(page_table, q, c_new, kr_new, wts["w_uk_t"], wts["w_uv"], gains["gk"], cache_c, cache_kr)


def _pool_prompt_kernel(u_ref, halo_ref, meta_ref, d_ref):
    i = pl.program_id(1)
    prev = jnp.where(i == 0, meta_ref[...], halo_ref[...])
    for g, w in enumerate(POOL_WINDOWS):
        cols = slice(g * POOL_GROUP_WIDTH, (g + 1) * POOL_GROUP_WIDTH)
        e = jnp.concatenate([prev[:, cols], u_ref[:, cols]], axis=0)
        win, span = e, 1
        while span < w:
            win = win + pltpu.roll(win, span, 0)
            span *= 2
        d_ref[:, cols] = (win[N_META:, :] * (1.0 / w) - e[N_META:, :]).astype(d_ref.dtype)


def _run_pool_prompt(u3, u_meta, *, tp):
    b, seq, _ = u3.shape
    assert seq % tp == 0 and tp % N_META == 0
    halo_blocks = tp // N_META
    return pl.pallas_call(
        _pool_prompt_kernel, grid=(b, seq // tp),
        in_specs=[pl.BlockSpec((None, tp, POOL_WIDTH), lambda bi, i: (bi, i, 0)),
                  pl.BlockSpec((None, N_META, POOL_WIDTH), lambda bi, i: (bi, jnp.maximum(i * halo_blocks - 1, 0), 0)),
                  pl.BlockSpec((N_META, POOL_WIDTH), lambda bi, i: (0, 0))],
        out_specs=pl.BlockSpec((None, tp, POOL_WIDTH), lambda bi, i: (bi, i, 0)),
        out_shape=jax.ShapeDtypeStruct((b, seq, POOL_WIDTH), BF16),
        compiler_params=_params(2), name="pool_prompt",
    )(u3, u3, u_meta)


def _pool_sample_kernel(e_ref, d_ref, *, s_new):
    for s in range(s_new):
        t = POOL_STATE + s
        for g, w in enumerate(POOL_WINDOWS):
            cols = slice(g * POOL_GROUP_WIDTH, (g + 1) * POOL_GROUP_WIDTH)
            cur = e_ref[t, :, cols]
            win = cur
            for j in range(1, w):
                win = win + e_ref[t - j, :, cols]
            d_ref[s, :, cols] = win * (1.0 / w) - cur


def _run_pool_sample(ext_t, *, s_new):
    t, n_seq, _ = ext_t.shape
    return pl.pallas_call(
        functools.partial(_pool_sample_kernel, s_new=s_new), grid=(1,),
        in_specs=[pl.BlockSpec((t, n_seq, POOL_WIDTH), lambda i: (0, 0, 0))],
        out_specs=pl.BlockSpec((s_new, n_seq, POOL_WIDTH), lambda i: (0, 0, 0)),
        out_shape=jax.ShapeDtypeStruct((s_new, n_seq, POOL_WIDTH), F32),
        compiler_params=_params(1), name="pool_sample",
    )(ext_t)


def _mlp_kernel(x_ref, d_ref, att_ref, gate_ref, w_pool_ref, pool_scale_ref, w_o_ref, g_mlp_ref,
                w_ff1_ref, w_ff2_ref, y_ref, *, ff_chunk):
    d = d_ref[...].astype(BF16)
    pool = jnp.concatenate(
        [_dot(d[:, g * POOL_GROUP_WIDTH:(g + 1) * POOL_GROUP_WIDTH], w_pool_ref[g]) for g in range(POOL_GROUPS)],
        axis=1) * pool_scale_ref[...]
    g_pool = gate_ref[:, :D_MODEL].astype(F32)
    g_att = gate_ref[:, D_MODEL:].astype(F32)
    mix = g_pool * pool + g_att * att_ref[...].astype(F32)
    h = x_ref[...] + _dot(mix.astype(BF16), w_o_ref[...])
    hn = _rms(h, g_mlp_ref[...]).astype(BF16)
    acc = h
    for c0 in range(0, D_FF, ff_chunk):
        z = jnp.maximum(_dot(hn, w_ff1_ref[:, c0:c0 + ff_chunk]), 0.0)
        acc = acc + _dot((z * z).astype(BF16), w_ff2_ref[c0:c0 + ff_chunk, :])
    y_ref[...] = acc


def _run_mlp(x2d, d2d, att2d, gate2d, wts, pool_scale, g_mlp, *, tm, ff_chunk=1024):
    rows = x2d.shape[0]
    assert rows % tm == 0 and D_FF % ff_chunk == 0
    row_spec = lambda w: pl.BlockSpec((tm, w), lambda i: (i, 0))
    return pl.pallas_call(
        functools.partial(_mlp_kernel, ff_chunk=ff_chunk), grid=(rows // tm,),
        in_specs=[row_spec(D_MODEL), row_spec(POOL_WIDTH), row_spec(D_MODEL), row_spec(2 * D_MODEL),
                  _const_spec((POOL_GROUPS, POOL_GROUP_WIDTH, POOL_GROUP_OUT)), _const_spec((1, D_MODEL)),
                  _const_spec((D_MODEL, D_MODEL)), _const_spec((1, D_MODEL)),
                  _const_spec((D_MODEL, D_FF)), _const_spec((D_FF, D_MODEL))],
        out_specs=row_spec(D_MODEL), out_shape=jax.ShapeDtypeStruct((rows, D_MODEL), F32),
        compiler_params=_params(1), name="mlp",
    )(x2d, d2d, att2d, gate2d, wts["w_pool"], pool_scale, wts["w_o"], g_mlp, wts["w_ff1"], wts["w_ff2"])


def _row_tile(rows, target):
    t = min(rows, target)
    while rows % t:
        t //= 2
    return t


def kernel(x_prompt, x_sample, cache_kv_latent, cache_k_rope, state_pool, page_table, meta_tokens, g_attn, w_in,
           g_q_lat, w_uq, g_kv_lat, g_qn_nope, g_qn_rope, w_uk, g_kn_nope, g_kn_rope, w_uv, w_pool_map, pool_scale,
           w_o, g_mlp, w_ff1, w_ff2):
    b, seq, _ = x_prompt.shape
    n_seq, s_new, _ = x_sample.shape
    n_pages = page_table.shape[1]
    past = n_pages * PAGE_SIZE

    wts = _prep_weights(w_in, w_uq, w_uk, w_uv, w_pool_map, w_o, w_ff1, w_ff2)
    row = lambda g: g.reshape(1, -1).astype(F32)
    gains = dict(g_attn=row(g_attn), g_q_lat=row(g_q_lat), g_kv_lat=row(g_kv_lat),
                 gq=_head_gain(g_qn_nope, g_qn_rope) * SCALE, gk=_head_gain(g_kn_nope, g_kn_rope))
    pool_scale = row(pool_scale)
    g_mlp = row(g_mlp)

    tm = _row_tile(seq, 512)
    tabs_meta = _rope_tables(jnp.arange(N_META))
    tabs_prompt = _rope_tables(N_META + jnp.arange(seq))
    tm_s = _row_tile(n_seq * s_new, 256)
    tabs_sample = _rope_tables(past + jnp.arange(tm_s) % s_new)

    u_m, _, c_m, kr_m, _, k_m, v_m = _run_proj(meta_tokens.astype(F32), tabs_meta, wts, gains,
                                               tm=N_META, with_kv=True, q_dtype=BF16)

    xp = x_prompt.reshape(b * seq, D_MODEL)
    u_p, q_p, c_p, kr_p, gate_p, k_p, v_p = _run_proj(xp, tabs_prompt, wts, gains, tm=tm, with_kv=True,
                                                      q_dtype=BF16)
    shape3 = lambda a: a.reshape(b, seq, a.shape[-1])
    att_p = _run_flash(shape3(q_p), shape3(k_p), shape3(v_p), k_m, v_m, tq=tm)
    d_p = _run_pool_prompt(shape3(u_p), u_m, tp=tm)
    y_p = _run_mlp(xp, d_p.reshape(b * seq, POOL_WIDTH), att_p.reshape(b * seq, D_MODEL), gate_p, wts,
                   pool_scale, g_mlp, tm=tm)

    xs = x_sample.reshape(n_seq * s_new, D_MODEL)
    u_s, q_s, c_s, kr_s, gate_s = _run_proj(xs, tabs_sample, wts, gains, tm=tm_s, with_kv=False, q_dtype=F32)
    att_s = _run_paged(page_table, q_s, c_s, kr_s, wts, gains, cache_kv_latent, cache_k_rope,
                       s_new=s_new, tk=_row_tile(past, 1024))
    ext_s = jnp.concatenate([state_pool.astype(F32), u_s.reshape(n_seq, s_new, POOL_WIDTH)], axis=1)
    d_s = _run_pool_sample(ext_s.transpose(1, 0, 2), s_new=s_new).transpose(1, 0, 2)
    y_s = _run_mlp(xs, d_s.reshape(n_seq * s_new, POOL_WIDTH), att_s, gate_s, wts, pool_scale, g_mlp, tm=tm_s)

    rep = lambda a: jnp.broadcast_to(a[None], (b,) + a.shape)
    return (
        y_p.reshape(b, seq, D_MODEL),
        y_s.reshape(n_seq, s_new, D_MODEL),
        jnp.concatenate([rep(c_m), shape3(c_p)], axis=1),
        jnp.concatenate([rep(kr_m), shape3(kr_p)], axis=1),
        shape3(u_p)[:, seq - POOL_STATE:],
        c_s.reshape(n_seq, s_new, KV_LORA),
        kr_s.reshape(n_seq, s_new, QK_ROPE),
        ext_s[:, s_new:],
    )
```

```python
import functools

import jax
import jax.numpy as jnp
from jax import lax
from jax.experimental import pallas as pl
from jax.experimental.pallas import tpu as pltpu

D_MODEL = 1024
N_META = 16
N_HEADS = 8
QK_NOPE = 64
QK_ROPE = 32
ROPE_HALF = QK_ROPE // 2
QK_HEAD = QK_NOPE + QK_ROPE
V_HEAD = D_MODEL // N_HEADS
Q_LORA = 3 * D_MODEL // 8
KV_LORA = D_MODEL // 4
POOL_WIDTH = D_MODEL // 2
POOL_WINDOWS = (2, 4, 8, 16)
POOL_GROUPS = len(POOL_WINDOWS)
POOL_GROUP_WIDTH = POOL_WIDTH // POOL_GROUPS
POOL_GROUP_OUT = D_MODEL // POOL_GROUPS
POOL_STATE = max(POOL_WINDOWS) - 1
D_FF = 4 * D_MODEL
ROPE_BASE = 10000.0
EPS = 1e-6
SCALE = QK_HEAD ** -0.5
PAGE_SIZE = 128

LANES = 128
V7X_VMEM_LIMIT_BYTES = 56 * 2 ** 20

HEAD_PAD = LANES
QK_PAD = N_HEADS * HEAD_PAD
KR_PAD = LANES
IN_U0, IN_Q0, IN_C0, IN_KR0 = 0, POOL_WIDTH, POOL_WIDTH + Q_LORA, POOL_WIDTH + Q_LORA + KV_LORA
IN_G0 = IN_KR0 + KR_PAD
IN_PAD = IN_G0 + 2 * D_MODEL
N_TABS = 4
PROJ_SUB_ROWS = 256
PAGED_GROUP_HEADS = 4

F32 = jnp.float32
BF16 = jnp.bfloat16


def _dot(a, b):
    return jnp.dot(a, b, preferred_element_type=F32)


def _dot_nt(a, b):
    return lax.dot_general(a, b, (((1,), (1,)), ((), ())), preferred_element_type=F32)


def _rms(x, g):
    return x * lax.rsqrt(jnp.mean(x * x, axis=-1, keepdims=True) + EPS) * g


def _const_spec(shape):
    zeros = (0,) * len(shape)
    return pl.BlockSpec(shape, lambda *_: zeros, pipeline_mode=pl.Buffered(1))


def _params(n_axes):
    return pltpu.CompilerParams(dimension_semantics=("arbitrary",) * n_axes,
                                vmem_limit_bytes=V7X_VMEM_LIMIT_BYTES)


def _rope_tables(pos):
    p = pos.shape[0]
    inv = ROPE_BASE ** (-jnp.arange(0, QK_ROPE, 2, dtype=F32) / QK_ROPE)
    ang = pos.astype(F32)[:, None] * inv[None, :]
    cos, sin = jnp.cos(ang), jnp.sin(ang)
    z = lambda n: jnp.zeros((p, n), F32)
    q_cos = jnp.concatenate([jnp.ones((p, QK_NOPE), F32), cos, cos, z(HEAD_PAD - QK_HEAD)], axis=1)
    q_sin = jnp.concatenate([z(QK_NOPE), -sin, sin, z(HEAD_PAD - QK_HEAD)], axis=1)
    half_cos = jnp.concatenate([cos, cos, z(KR_PAD // 2 - QK_ROPE)], axis=1)
    half_sin = jnp.concatenate([-sin, sin, z(KR_PAD // 2 - QK_ROPE)], axis=1)
    return jnp.concatenate([q_cos, q_sin, half_cos, half_cos, half_sin, half_sin], axis=1)


def _head_gain(g_nope, g_rope):
    g = jnp.concatenate([g_nope, g_rope, g_rope, jnp.zeros((HEAD_PAD - QK_HEAD,), F32)])
    return g.reshape(1, HEAD_PAD).astype(F32)


def _prep_weights(w_in, w_uq, w_uk, w_uv, w_pool_map, w_o, w_ff1, w_ff2):
    w_kr = w_in[:, IN_KR0:IN_KR0 + QK_ROPE]
    kr_half = jnp.concatenate([w_kr, w_kr[:, :ROPE_HALF], jnp.zeros((D_MODEL, ROPE_HALF), w_in.dtype)], axis=1)
    w_in_pad = jnp.concatenate([w_in[:, :IN_KR0], kr_half, kr_half, w_in[:, IN_KR0 + QK_ROPE:]],
                               axis=1).astype(BF16)
    uq = w_uq.reshape(Q_LORA, N_HEADS, QK_HEAD)
    uq = jnp.concatenate([uq, uq[:, :, QK_NOPE:QK_NOPE + ROPE_HALF],
                          jnp.zeros((Q_LORA, N_HEADS, HEAD_PAD - QK_HEAD - ROPE_HALF), uq.dtype)], axis=2)
    uq = uq.reshape(Q_LORA, QK_PAD).astype(BF16)
    uk = jnp.pad(w_uk, ((0, 0), (0, 0), (0, HEAD_PAD - QK_NOPE))).reshape(KV_LORA, QK_PAD).astype(BF16)
    uk_t = w_uk.reshape(KV_LORA, N_HEADS * QK_NOPE).T.astype(BF16)
    uv = w_uv.reshape(KV_LORA, N_HEADS * V_HEAD).astype(BF16)
    return dict(w_in=w_in_pad, w_uq=uq, w_uk=uk, w_uk_t=uk_t, w_uv=uv, w_pool=w_pool_map.astype(BF16),
                w_o=w_o.astype(BF16), w_ff1=w_ff1.astype(BF16), w_ff2=w_ff2.astype(BF16))


def _rope_group(x, cos, sin):
    return x * cos + pltpu.roll(x, LANES - ROPE_HALF, 1) * sin


def _norm_head(x, gain):
    ssq = jnp.sum(x * x, axis=-1, keepdims=True)
    return x * lax.rsqrt(ssq * (1.0 / QK_HEAD) + EPS) * gain


def _proj_rows(rows, x_ref, tab_ref, g_attn_ref, w_in_ref, g_q_ref, w_uq_ref, g_kv_ref, gq_ref,
               w_uk_ref, gk_ref, w_uv_ref, u_ref, q_ref, c_ref, kr_ref, gate_ref, kv_refs):
    xn = _rms(x_ref[rows, :], g_attn_ref[...]).astype(BF16)
    u_ref[rows, :] = _dot(xn, w_in_ref[:, IN_U0:IN_Q0])

    c = _rms(_dot(xn, w_in_ref[:, IN_C0:IN_KR0]), g_kv_ref[...])
    c_ref[rows, :] = c

    tab = lambda t: tab_ref[rows, t * LANES:(t + 1) * LANES]
    kr = _rope_group(_dot(xn, w_in_ref[:, IN_KR0:IN_G0]), tab(2), tab(3))
    kr_ref[rows, :] = kr[:, :QK_ROPE]

    q_lat = _rms(_dot(xn, w_in_ref[:, IN_Q0:IN_C0]), g_q_ref[...]).astype(BF16)
    q_full = _dot(q_lat, w_uq_ref[...])
    q_gain = gq_ref[...]
    for h in range(N_HEADS):
        sl = slice(h * HEAD_PAD, (h + 1) * HEAD_PAD)
        qh = _rope_group(q_full[:, sl], tab(0), tab(1))
        q_ref[rows, sl] = _norm_head(qh, q_gain).astype(q_ref.dtype)

    if kv_refs:
        k_ref, v_ref = kv_refs
        cb = c.astype(BF16)
        k_full = _dot(cb, w_uk_ref[...])
        lane = lax.broadcasted_iota(jnp.int32, kr.shape, 1)
        kr_hi = jnp.where(lane >= QK_NOPE, kr, 0.0)
        k_gain = gk_ref[...]
        for h in range(N_HEADS):
            sl = slice(h * HEAD_PAD, (h + 1) * HEAD_PAD)
            k_ref[rows, sl] = _norm_head(k_full[:, sl] + kr_hi, k_gain).astype(k_ref.dtype)
        v_ref[rows, :] = _dot(cb, w_uv_ref[...]).astype(v_ref.dtype)

    gate = _dot(xn, w_in_ref[:, IN_G0:IN_PAD])
    gate_ref[rows, :] = jax.nn.sigmoid(gate).astype(gate_ref.dtype)


def _proj_kernel(*refs, sub):
    n_in = 11
    ins, outs = refs[:n_in], refs[n_in:]
    tm = ins[0].shape[0]
    for r in range(tm // sub):
        _proj_rows(slice(r * sub, (r + 1) * sub), *ins, *outs[:5], outs[5:])


def _run_proj(x2d, tabs, wts, gains, *, tm, with_kv, q_dtype):
    rows = x2d.shape[0]
    assert rows % tm == 0 and tabs.shape[0] % tm == 0
    tab_blocks = tabs.shape[0] // tm
    row_spec = lambda w: pl.BlockSpec((tm, w), lambda i: (i, 0))
    in_specs = [
        row_spec(D_MODEL),
        pl.BlockSpec((tm, N_TABS * LANES), lambda i: (i % tab_blocks, 0)),
        _const_spec((1, D_MODEL)), _const_spec((D_MODEL, IN_PAD)),
        _const_spec((1, Q_LORA)), _const_spec((Q_LORA, QK_PAD)),
        _const_spec((1, KV_LORA)), _const_spec((1, HEAD_PAD)),
        _const_spec((KV_LORA, QK_PAD)), _const_spec((1, HEAD_PAD)), _const_spec((KV_LORA, QK_PAD)),
    ]
    out_shape = [
        jax.ShapeDtypeStruct((rows, POOL_WIDTH), F32),
        jax.ShapeDtypeStruct((rows, QK_PAD), q_dtype),
        jax.ShapeDtypeStruct((rows, KV_LORA), F32),
        jax.ShapeDtypeStruct((rows, QK_ROPE), F32),
        jax.ShapeDtypeStruct((rows, 2 * D_MODEL), BF16),
    ]
    out_specs = [row_spec(POOL_WIDTH), row_spec(QK_PAD), row_spec(KV_LORA), row_spec(QK_ROPE),
                 row_spec(2 * D_MODEL)]
    if with_kv:
        out_shape += [jax.ShapeDtypeStruct((rows, QK_PAD), BF16)] * 2
        out_specs += [row_spec(QK_PAD)] * 2
    return pl.pallas_call(
        functools.partial(_proj_kernel, sub=min(tm, PROJ_SUB_ROWS)),
        grid=(rows // tm,), in_specs=in_specs, out_specs=out_specs, out_shape=out_shape,
        compiler_params=_params(1), name="proj",
    )(x2d, tabs, gains["g_attn"], wts["w_in"], gains["g_q_lat"], wts["w_uq"], gains["g_kv_lat"],
      gains["gq"], wts["w_uk"], gains["gk"], wts["w_uv"])


def _softmax_step(s, v, m, l, acc):
    m_new = jnp.maximum(m, jnp.max(s, axis=-1, keepdims=True))
    alpha = jnp.exp(m - m_new)
    p = jnp.exp(s - m_new)
    l = alpha * l + jnp.sum(p, axis=-1, keepdims=True)
    acc = alpha * acc + _dot(p.astype(BF16), v)
    return m_new, l, acc


def _flash_kernel(q_ref, k_ref, v_ref, km_ref, vm_ref, o_ref, *, tq):
    k_meta, v_meta = km_ref[...], vm_ref[...]
    causal = lax.broadcasted_iota(jnp.int32, (tq, tq), 1) <= lax.broadcasted_iota(jnp.int32, (tq, tq), 0)
    for i in range(q_ref.shape[0] // tq):
        rows = slice(i * tq, (i + 1) * tq)
        q = q_ref[rows, :]
        s = _dot_nt(q, k_meta)
        m = jnp.max(s, axis=-1, keepdims=True)
        p = jnp.exp(s - m)
        l = jnp.sum(p, axis=-1, keepdims=True)
        acc = _dot(p.astype(BF16), v_meta)
        for j in range(i):
            keys = slice(j * tq, (j + 1) * tq)
            m, l, acc = _softmax_step(_dot_nt(q, k_ref[keys, :]), v_ref[keys, :], m, l, acc)
        s = jnp.where(causal, _dot_nt(q, k_ref[rows, :]), -jnp.inf)
        m, l, acc = _softmax_step(s, v_ref[rows, :], m, l, acc)
        o_ref[rows, :] = (acc / l).astype(o_ref.dtype)


def _run_flash(q, k, v, k_meta, v_meta, *, tq):
    b, seq, _ = q.shape
    assert seq % tq == 0
    head_spec = pl.BlockSpec((None, seq, HEAD_PAD), lambda bi, h: (bi, 0, h))
    meta_spec = pl.BlockSpec((N_META, HEAD_PAD), lambda bi, h: (0, h))
    return pl.pallas_call(
        functools.partial(_flash_kernel, tq=tq),
        grid=(b, N_HEADS),
        in_specs=[head_spec, head_spec, head_spec, meta_spec, meta_spec],
        out_specs=head_spec, out_shape=jax.ShapeDtypeStruct((b, seq, N_HEADS * V_HEAD), BF16),
        compiler_params=_params(2), name="flash",
    )(q, k, v, k_meta, v_meta)


def _paged_kernel(pt_ref, q_ref, cn_ref, krn_ref, w_uk_t_ref, w_uv_ref, gk_ref, cache_c_ref, cache_kr_ref,
                  o_ref, c_buf, kr_buf, sem, a_buf, qr_buf, cnew_buf, *, n_pages, n_seq, s_new, tk):
    b = pl.program_id(0)
    slot = b % 2
    rows_hq = N_HEADS * s_new

    def page_copies(seq, sl, j):
        page = pt_ref[seq, j]
        dst = pl.ds(pl.multiple_of(j * PAGE_SIZE, PAGE_SIZE), PAGE_SIZE)
        return (pltpu.make_async_copy(cache_c_ref.at[page], c_buf.at[sl, dst, :], sem.at[0, sl]),
                pltpu.make_async_copy(cache_kr_ref.at[page], kr_buf.at[sl, :, dst], sem.at[1, sl]))

    def start_seq(seq, sl):
        def body(j, _):
            for cp in page_copies(seq, sl, j):
                cp.start()
            return 0
        lax.fori_loop(0, n_pages, body, 0)

    def wait_seq(seq, sl):
        def body(j, _):
            for cp in page_copies(seq, sl, j):
                cp.wait()
            return 0
        lax.fori_loop(0, n_pages, body, 0)

    grp_nope = PAGED_GROUP_HEADS * QK_NOPE
    grp_rows = grp_nope + PAGED_GROUP_HEADS * s_new

    @pl.when(b == 0)
    def _():
        start_seq(0, 0)
        for g in range(N_HEADS // PAGED_GROUP_HEADS):
            a_buf[g * grp_rows:g * grp_rows + grp_nope, :] = w_uk_t_ref[g * grp_nope:(g + 1) * grp_nope, :]
        cnew_buf[...] = jnp.zeros_like(cnew_buf)

    wait_seq(b, slot)

    @pl.when(b + 1 < n_seq)
    def _():
        start_seq(b + 1, 1 - slot)

    qg = q_ref[...] * jnp.concatenate([gk_ref[...]] * N_HEADS, axis=1)
    for h in range(N_HEADS):
        lo = h * HEAD_PAD
        q_nope = qg[:, lo:lo + QK_NOPE].astype(BF16)
        q_abs = _dot(q_nope, w_uk_t_ref[h * QK_NOPE:(h + 1) * QK_NOPE, :])
        g, hh = divmod(h, PAGED_GROUP_HEADS)
        row0 = g * grp_rows + grp_nope + hh * s_new
        a_buf[row0:row0 + s_new, :] = q_abs.astype(BF16)
        qr_buf[h * s_new:(h + 1) * s_new, :] = qg[:, lo + QK_NOPE:lo + QK_HEAD]
    cnew_buf[0:s_new, :] = cn_ref[...]

    a_mats = [a_buf[g * grp_rows:(g + 1) * grp_rows, :] for g in range(N_HEADS // PAGED_GROUP_HEADS)]
    q_rope = qr_buf[...].astype(BF16)

    def tile(c_t, kr_t, carry, mask):
        m, l, acc = carry
        c_bf = c_t.astype(BF16)
        r = [_dot_nt(a, c_bf) for a in a_mats]
        s_rope = _dot(q_rope, kr_t.astype(BF16))
        ssq_rope = jnp.sum(kr_t * kr_t, axis=0, keepdims=True)
        parts = []
        for h in range(N_HEADS):
            g, hh = divmod(h, PAGED_GROUP_HEADS)
            kn = r[g][hh * QK_NOPE:(hh + 1) * QK_NOPE, :]
            ssq = jnp.sum(kn * kn, axis=0, keepdims=True) + ssq_rope
            rinv = lax.rsqrt(ssq * (1.0 / QK_HEAD) + EPS)
            s_nope = r[g][grp_nope + hh * s_new:grp_nope + (hh + 1) * s_new, :]
            parts.append((s_nope + s_rope[h * s_new:(h + 1) * s_new, :]) * rinv)
        s = jnp.concatenate(parts, axis=0)
        if mask is not None:
            s = jnp.where(mask, s, -jnp.inf)
        return _softmax_step(s, c_bf, m, l, acc)

    carry = (jnp.full((rows_hq, 1), -jnp.inf, F32), jnp.zeros((rows_hq, 1), F32),
             jnp.zeros((rows_hq, KV_LORA), F32))

    for t in range(n_pages * PAGE_SIZE // tk):
        carry = tile(c_buf[slot, t * tk:(t + 1) * tk, :], kr_buf[slot, :, t * tk:(t + 1) * tk], carry, None)

    shape = (rows_hq, cnew_buf.shape[0])
    tok = lax.broadcasted_iota(jnp.int32, shape, 0) % s_new
    mask = lax.broadcasted_iota(jnp.int32, shape, 1) <= tok
    m, l, acc = tile(cnew_buf[...], krn_ref[...], carry, mask)

    ctx = (acc / l).astype(BF16)
    for h in range(N_HEADS):
        o_ref[:, h * V_HEAD:(h + 1) * V_HEAD] = _dot(ctx[h * s_new:(h + 1) * s_new, :],
                                                     w_uv_ref[:, h * V_HEAD:(h + 1) * V_HEAD])


def _run_paged(page_table, q, c_new, kr_new, wts, gains, cache_c, cache_kr, *, s_new, tk):
    n_seq, n_pages = page_table.shape
    past = n_pages * PAGE_SIZE
    assert past % tk == 0 and s_new % 8 == 0 and s_new <= PAGE_SIZE
    rows_hq = N_HEADS * s_new
    seq_spec = lambda w: pl.BlockSpec((s_new, w), lambda b, pt: (b, 0))
    const = lambda shape: pl.BlockSpec(shape, lambda b, pt: (0,) * len(shape), pipeline_mode=pl.Buffered(1))
    grid_spec = pltpu.PrefetchScalarGridSpec(
        num_scalar_prefetch=1, grid=(n_seq,),
        in_specs=[seq_spec(QK_PAD), seq_spec(KV_LORA),
                  pl.BlockSpec((None, QK_ROPE, PAGE_SIZE), lambda b, pt: (b, 0, 0)),
                  const((N_HEADS * QK_NOPE, KV_LORA)), const((KV_LORA, N_HEADS * V_HEAD)), const((1, HEAD_PAD)),
                  pl.BlockSpec(memory_space=pl.ANY), pl.BlockSpec(memory_space=pl.ANY)],
        out_specs=seq_spec(N_HEADS * V_HEAD),
        scratch_shapes=[
            pltpu.VMEM((2, past, KV_LORA), F32),
            pltpu.VMEM((2, QK_ROPE, past), F32),
            pltpu.SemaphoreType.DMA((2, 2)),
            pltpu.VMEM((N_HEADS * QK_NOPE + rows_hq, KV_LORA), BF16),
            pltpu.VMEM((rows_hq, QK_ROPE), F32),
            pltpu.VMEM((PAGE_SIZE, KV_LORA), F32),
        ])
    return pl.pallas_call(
        functools.partial(_paged_kernel, n_pages=n_pages, n_seq=n_seq, s_new=s_new, tk=tk),
        grid_spec=grid_spec, out_shape=jax.ShapeDtypeStruct((n_seq * s_new, N_HEADS * V_HEAD), F32),
        compiler_params=_params(1), name="paged",
    )(page_table, q, c_new, kr_new, wts["w_uk_t"], wts["w_uv"], gains["gk"], cache_c, cache_kr)


def _pool_prompt_kernel(u_ref, halo_ref, meta_ref, d_ref):
    i = pl.program_id(1)
    prev = jnp.where(i == 0, meta_ref[...], halo_ref[...])
    for g, w in enumerate(POOL_WINDOWS):
        cols = slice(g * POOL_GROUP_WIDTH, (g + 1) * POOL_GROUP_WIDTH)
        e = jnp.concatenate([prev[:, cols], u_ref[:, cols]], axis=0)
        win, span = e, 1
        while span < w:
            win = win + pltpu.roll(win, span, 0)
            span *= 2
        d_ref[:, cols] = (win[N_META:, :] * (1.0 / w) - e[N_META:, :]).astype(d_ref.dtype)


def _run_pool_prompt(u3, u_meta, *, tp):
    b, seq, _ = u3.shape
    assert seq % tp == 0 and tp % N_META == 0
    halo_blocks = tp // N_META
    return pl.pallas_call(
        _pool_prompt_kernel, grid=(b, seq // tp),
        in_specs=[pl.BlockSpec((None, tp, POOL_WIDTH), lambda bi, i: (bi, i, 0)),
                  pl.BlockSpec((None, N_META, POOL_WIDTH), lambda bi, i: (bi, jnp.maximum(i * halo_blocks - 1, 0), 0)),
                  pl.BlockSpec((N_META, POOL_WIDTH), lambda bi, i: (0, 0))],
        out_specs=pl.BlockSpec((None, tp, POOL_WIDTH), lambda bi, i: (bi, i, 0)),
        out_shape=jax.ShapeDtypeStruct((b, seq, POOL_WIDTH), BF16),
        compiler_params=_params(2), name="pool_prompt",
    )(u3, u3, u_meta)


def _pool_sample_kernel(e_ref, d_ref, *, s_new):
    for s in range(s_new):
        t = POOL_STATE + s
        for g, w in enumerate(POOL_WINDOWS):
            cols = slice(g * POOL_GROUP_WIDTH, (g + 1) * POOL_GROUP_WIDTH)
            cur = e_ref[t, :, cols]
            win = cur
            for j in range(1, w):
                win = win + e_ref[t - j, :, cols]
            d_ref[s, :, cols] = win * (1.0 / w) - cur


def _run_pool_sample(ext_t, *, s_new):
    t, n_seq, _ = ext_t.shape
    return pl.pallas_call(
        functools.partial(_pool_sample_kernel, s_new=s_new), grid=(1,),
        in_specs=[pl.BlockSpec((t, n_seq, POOL_WIDTH), lambda i: (0, 0, 0))],
        out_specs=pl.BlockSpec((s_new, n_seq, POOL_WIDTH), lambda i: (0, 0, 0)),
        out_shape=jax.ShapeDtypeStruct((s_new, n_seq, POOL_WIDTH), F32),
        compiler_params=_params(1), name="pool_sample",
    )(ext_t)


def _mlp_kernel(x_ref, d_ref, att_ref, gate_ref, w_pool_ref, pool_scale_ref, w_o_ref, g_mlp_ref,
                w_ff1_ref, w_ff2_ref, y_ref, *, ff_chunk):
    d = d_ref[...].astype(BF16)
    pool = jnp.concatenate(
        [_dot(d[:, g * POOL_GROUP_WIDTH:(g + 1) * POOL_GROUP_WIDTH], w_pool_ref[g]) for g in range(POOL_GROUPS)],
        axis=1) * pool_scale_ref[...]
    g_pool = gate_ref[:, :D_MODEL].astype(F32)
    g_att = gate_ref[:, D_MODEL:].astype(F32)
    mix = g_pool * pool + g_att * att_ref[...].astype(F32)
    h = x_ref[...] + _dot(mix.astype(BF16), w_o_ref[...])
    hn = _rms(h, g_mlp_ref[...]).astype(BF16)
    acc = h
    for c0 in range(0, D_FF, ff_chunk):
        z = jnp.maximum(_dot(hn, w_ff1_ref[:, c0:c0 + ff_chunk]), 0.0)
        acc = acc + _dot((z * z).astype(BF16), w_ff2_ref[c0:c0 + ff_chunk, :])
    y_ref[...] = acc


def _run_mlp(x2d, d2d, att2d, gate2d, wts, pool_scale, g_mlp, *, tm, ff_chunk=1024):
    rows = x2d.shape[0]
    assert rows % tm == 0 and D_FF % ff_chunk == 0
    row_spec = lambda w: pl.BlockSpec((tm, w), lambda i: (i, 0))
    return pl.pallas_call(
        functools.partial(_mlp_kernel, ff_chunk=ff_chunk), grid=(rows // tm,),
        in_specs=[row_spec(D_MODEL), row_spec(POOL_WIDTH), row_spec(D_MODEL), row_spec(2 * D_MODEL),
                  _const_spec((POOL_GROUPS, POOL_GROUP_WIDTH, POOL_GROUP_OUT)), _const_spec((1, D_MODEL)),
                  _const_spec((D_MODEL, D_MODEL)), _const_spec((1, D_MODEL)),
                  _const_spec((D_MODEL, D_FF)), _const_spec((D_FF, D_MODEL))],
        out_specs=row_spec(D_MODEL), out_shape=jax.ShapeDtypeStruct((rows, D_MODEL), F32),
        compiler_params=_params(1), name="mlp",
    )(x2d, d2d, att2d, gate2d, wts["w_pool"], pool_scale, wts["w_o"], g_mlp, wts["w_ff1"], wts["w_ff2"])


def _row_tile(rows, target):
    t = min(rows, target)
    while rows % t:
        t //= 2
    return t


def kernel(x_prompt, x_sample, cache_kv_latent, cache_k_rope, state_pool, page_table, meta_tokens, g_attn, w_in,
           g_q_lat, w_uq, g_kv_lat, g_qn_nope, g_qn_rope, w_uk, g_kn_nope, g_kn_rope, w_uv, w_pool_map, pool_scale,
           w_o, g_mlp, w_ff1, w_ff2):
    b, seq, _ = x_prompt.shape
    n_seq, s_new, _ = x_sample.shape
    n_pages = page_table.shape[1]
    past = n_pages * PAGE_SIZE

    wts = _prep_weights(w_in, w_uq, w_uk, w_uv, w_pool_map, w_o, w_ff1, w_ff2)
    row = lambda g: g.reshape(1, -1).astype(F32)
    gains = dict(g_attn=row(g_attn), g_q_lat=row(g_q_lat), g_kv_lat=row(g_kv_lat),
                 gq=_head_gain(g_qn_nope, g_qn_rope) * SCALE, gk=_head_gain(g_kn_nope, g_kn_rope))
    pool_scale = row(pool_scale)
    g_mlp = row(g_mlp)

    tm = _row_tile(seq, 512)
    tabs_meta = _rope_tables(jnp.arange(N_META))
    tabs_prompt = _rope_tables(N_META + jnp.arange(seq))
    tm_s = _row_tile(n_seq * s_new, 256)
    tabs_sample = _rope_tables(past + jnp.arange(tm_s) % s_new)

    u_m, _, c_m, kr_m, _, k_m, v_m = _run_proj(meta_tokens.astype(F32), tabs_meta, wts, gains,
                                               tm=N_META, with_kv=True, q_dtype=BF16)

    xp = x_prompt.reshape(b * seq, D_MODEL)
    u_p, q_p, c_p, kr_p, gate_p, k_p, v_p = _run_proj(xp, tabs_prompt, wts, gains, tm=tm, with_kv=True,
                                                      q_dtype=BF16)
    shape3 = lambda a: a.reshape(b, seq, a.shape[-1])
    att_p = _run_flash(shape3(q_p), shape3(k_p), shape3(v_p), k_m, v_m, tq=tm)
    d_p = _run_pool_prompt(shape3(u_p), u_m, tp=tm)
    y_p = _run_mlp(xp, d_p.reshape(b * seq, POOL_WIDTH), att_p.reshape(b * seq, D_MODEL), gate_p, wts,
                   pool_scale, g_mlp, tm=tm)

    xs = x_sample.reshape(n_seq * s_new, D_MODEL)
    u_s, q_s, c_s, kr_s, gate_s = _run_proj(xs, tabs_sample, wts, gains, tm=tm_s, with_kv=False, q_dtype=F32)
    kr_new_t = jnp.pad(kr_s.reshape(n_seq, s_new, QK_ROPE).transpose(0, 2, 1),
                       ((0, 0), (0, 0), (0, PAGE_SIZE - s_new)))
    att_s = _run_paged(page_table, q_s, c_s, kr_new_t, wts, gains, cache_kv_latent,
                       cache_k_rope.transpose(0, 2, 1), s_new=s_new, tk=_row_tile(past, 1024))
    ext_s = jnp.concatenate([state_pool.astype(F32), u_s.reshape(n_seq, s_new, POOL_WIDTH)], axis=1)
    d_s = _run_pool_sample(ext_s.transpose(1, 0, 2), s_new=s_new).transpose(1, 0, 2)
    y_s = _run_mlp(xs, d_s.reshape(n_seq * s_new, POOL_WIDTH), att_s, gate_s, wts, pool_scale, g_mlp, tm=tm_s)

    rep = lambda a: jnp.broadcast_to(a[None], (b,) + a.shape)
    return (
        y_p.reshape(b, seq, D_MODEL),
        y_s.reshape(n_seq, s_new, D_MODEL),
        jnp.concatenate([rep(c_m), shape3(c_p)], axis=1),
        jnp.concatenate([rep(kr_m), shape3(kr_p)], axis=1),
        shape3(u_p)[:, seq - POOL_STATE:],
        c_s.reshape(n_seq, s_new, KV_LORA),
        kr_s.reshape(n_seq, s_new, QK_ROPE),
        ext_s[:, s_new:],
    )
```

```python
import functools

import jax
import jax.numpy as jnp
from jax import lax
from jax.experimental import pallas as pl
from jax.experimental.pallas import tpu as pltpu

D_MODEL = 1024
N_META = 16
N_HEADS = 8
QK_NOPE = 64
QK_ROPE = 32
ROPE_HALF = QK_ROPE // 2
QK_HEAD = QK_NOPE + QK_ROPE
V_HEAD = D_MODEL // N_HEADS
Q_LORA = 3 * D_MODEL // 8
KV_LORA = D_MODEL // 4
POOL_WIDTH = D_MODEL // 2
POOL_WINDOWS = (2, 4, 8, 16)
POOL_GROUPS = len(POOL_WINDOWS)
POOL_GROUP_WIDTH = POOL_WIDTH // POOL_GROUPS
POOL_GROUP_OUT = D_MODEL // POOL_GROUPS
POOL_STATE = max(POOL_WINDOWS) - 1
D_FF = 4 * D_MODEL
ROPE_BASE = 10000.0
EPS = 1e-6
SCALE = QK_HEAD ** -0.5
LOG2_E = 1.4426950408889634
PAGE_SIZE = 128

LANES = 128
V7X_VMEM_LIMIT_BYTES = 56 * 2 ** 20

HEAD_PAD = LANES
QK_PAD = N_HEADS * HEAD_PAD
KR_PAD = LANES
IN_U0, IN_Q0, IN_C0, IN_KR0 = 0, POOL_WIDTH, POOL_WIDTH + Q_LORA, POOL_WIDTH + Q_LORA + KV_LORA
IN_G0 = IN_KR0 + KR_PAD
IN_PAD = IN_G0 + 2 * D_MODEL
N_TABS = 4
PROJ_SUB_ROWS = 256
PAGED_GROUP_HEADS = 4

F32 = jnp.float32
BF16 = jnp.bfloat16


def _dot(a, b):
    return jnp.dot(a, b, preferred_element_type=F32)


def _dot_nt(a, b):
    return lax.dot_general(a, b, (((1,), (1,)), ((), ())), preferred_element_type=F32)


def _rms(x, g):
    return x * lax.rsqrt(jnp.mean(x * x, axis=-1, keepdims=True) + EPS) * g


def _const_spec(shape):
    zeros = (0,) * len(shape)
    return pl.BlockSpec(shape, lambda *_: zeros, pipeline_mode=pl.Buffered(1))


def _params(n_axes):
    return pltpu.CompilerParams(dimension_semantics=("arbitrary",) * n_axes,
                                vmem_limit_bytes=V7X_VMEM_LIMIT_BYTES)


def _rope_tables(pos):
    p = pos.shape[0]
    inv = ROPE_BASE ** (-jnp.arange(0, QK_ROPE, 2, dtype=F32) / QK_ROPE)
    ang = pos.astype(F32)[:, None] * inv[None, :]
    cos, sin = jnp.cos(ang), jnp.sin(ang)
    z = lambda n: jnp.zeros((p, n), F32)
    q_cos = jnp.concatenate([jnp.ones((p, QK_NOPE), F32), cos, cos, z(HEAD_PAD - QK_HEAD)], axis=1)
    q_sin = jnp.concatenate([z(QK_NOPE), -sin, sin, z(HEAD_PAD - QK_HEAD)], axis=1)
    half_cos = jnp.concatenate([cos, cos, z(KR_PAD // 2 - QK_ROPE)], axis=1)
    half_sin = jnp.concatenate([-sin, sin, z(KR_PAD // 2 - QK_ROPE)], axis=1)
    return jnp.concatenate([q_cos, q_sin, half_cos, half_cos, half_sin, half_sin], axis=1)


def _head_gain(g_nope, g_rope):
    g = jnp.concatenate([g_nope, g_rope, g_rope, jnp.zeros((HEAD_PAD - QK_HEAD,), F32)])
    return g.reshape(1, HEAD_PAD).astype(F32)


def _prep_weights(w_in, w_uq, w_uk, w_uv, w_pool_map, w_o, w_ff1, w_ff2):
    w_kr = w_in[:, IN_KR0:IN_KR0 + QK_ROPE]
    kr_half = jnp.concatenate([w_kr, w_kr[:, :ROPE_HALF], jnp.zeros((D_MODEL, ROPE_HALF), w_in.dtype)], axis=1)
    w_in_pad = jnp.concatenate([w_in[:, :IN_KR0], kr_half, kr_half, w_in[:, IN_KR0 + QK_ROPE:]],
                               axis=1).astype(BF16)
    uq = w_uq.reshape(Q_LORA, N_HEADS, QK_HEAD)
    uq = jnp.concatenate([uq, uq[:, :, QK_NOPE:QK_NOPE + ROPE_HALF],
                          jnp.zeros((Q_LORA, N_HEADS, HEAD_PAD - QK_HEAD - ROPE_HALF), uq.dtype)], axis=2)
    uq = uq.reshape(Q_LORA, QK_PAD).astype(BF16)
    uk = jnp.pad(w_uk, ((0, 0), (0, 0), (0, HEAD_PAD - QK_NOPE))).reshape(KV_LORA, QK_PAD).astype(BF16)
    uk_t = w_uk.reshape(KV_LORA, N_HEADS * QK_NOPE).T.astype(BF16)
    uv = w_uv.reshape(KV_LORA, N_HEADS * V_HEAD).astype(BF16)
    return dict(w_in=w_in_pad, w_uq=uq, w_uk=uk, w_uk_t=uk_t, w_uv=uv, w_pool=w_pool_map.astype(BF16),
                w_o=w_o.astype(BF16), w_ff1=w_ff1.astype(BF16), w_ff2=w_ff2.astype(BF16))


def _rope_group(x, cos, sin):
    return x * cos + pltpu.roll(x, LANES - ROPE_HALF, 1) * sin


def _norm_head(x, gain):
    ssq = jnp.sum(x * x, axis=-1, keepdims=True)
    return x * lax.rsqrt(ssq * (1.0 / QK_HEAD) + EPS) * gain


def _proj_rows(rows, x_ref, tab_ref, g_attn_ref, w_in_ref, g_q_ref, w_uq_ref, g_kv_ref, gq_ref,
               w_uk_ref, gk_ref, w_uv_ref, u_ref, q_ref, c_ref, kr_ref, gate_ref, kv_refs):
    xn = _rms(x_ref[rows, :], g_attn_ref[...]).astype(BF16)
    u_ref[rows, :] = _dot(xn, w_in_ref[:, IN_U0:IN_Q0])

    c = _rms(_dot(xn, w_in_ref[:, IN_C0:IN_KR0]), g_kv_ref[...])
    c_ref[rows, :] = c

    tab = lambda t: tab_ref[rows, t * LANES:(t + 1) * LANES]
    kr = _rope_group(_dot(xn, w_in_ref[:, IN_KR0:IN_G0]), tab(2), tab(3))
    kr_ref[rows, :] = kr[:, :QK_ROPE]

    q_lat = _rms(_dot(xn, w_in_ref[:, IN_Q0:IN_C0]), g_q_ref[...]).astype(BF16)
    q_full = _dot(q_lat, w_uq_ref[...])
    q_gain = gq_ref[...]
    for h in range(N_HEADS):
        sl = slice(h * HEAD_PAD, (h + 1) * HEAD_PAD)
        qh = _rope_group(q_full[:, sl], tab(0), tab(1))
        q_ref[rows, sl] = _norm_head(qh, q_gain).astype(q_ref.dtype)

    if kv_refs:
        k_ref, v_ref = kv_refs
        cb = c.astype(BF16)
        k_full = _dot(cb, w_uk_ref[...])
        lane = lax.broadcasted_iota(jnp.int32, kr.shape, 1)
        kr_hi = jnp.where(lane >= QK_NOPE, kr, 0.0)
        k_gain = gk_ref[...]
        for h in range(N_HEADS):
            sl = slice(h * HEAD_PAD, (h + 1) * HEAD_PAD)
            k_ref[rows, sl] = _norm_head(k_full[:, sl] + kr_hi, k_gain).astype(k_ref.dtype)
        v_ref[rows, :] = _dot(cb, w_uv_ref[...]).astype(v_ref.dtype)

    gate = _dot(xn, w_in_ref[:, IN_G0:IN_PAD])
    gate_ref[rows, :] = jax.nn.sigmoid(gate).astype(gate_ref.dtype)


def _proj_kernel(*refs, sub):
    n_in = 11
    ins, outs = refs[:n_in], refs[n_in:]
    tm = ins[0].shape[0]
    for r in range(tm // sub):
        _proj_rows(slice(r * sub, (r + 1) * sub), *ins, *outs[:5], outs[5:])


def _run_proj(x2d, tabs, wts, gains, *, tm, with_kv, q_dtype):
    rows = x2d.shape[0]
    assert rows % tm == 0 and tabs.shape[0] % tm == 0
    tab_blocks = tabs.shape[0] // tm
    row_spec = lambda w: pl.BlockSpec((tm, w), lambda i: (i, 0))
    in_specs = [
        row_spec(D_MODEL),
        pl.BlockSpec((tm, N_TABS * LANES), lambda i: (i % tab_blocks, 0)),
        _const_spec((1, D_MODEL)), _const_spec((D_MODEL, IN_PAD)),
        _const_spec((1, Q_LORA)), _const_spec((Q_LORA, QK_PAD)),
        _const_spec((1, KV_LORA)), _const_spec((1, HEAD_PAD)),
        _const_spec((KV_LORA, QK_PAD)), _const_spec((1, HEAD_PAD)), _const_spec((KV_LORA, QK_PAD)),
    ]
    out_shape = [
        jax.ShapeDtypeStruct((rows, POOL_WIDTH), F32),
        jax.ShapeDtypeStruct((rows, QK_PAD), q_dtype),
        jax.ShapeDtypeStruct((rows, KV_LORA), F32),
        jax.ShapeDtypeStruct((rows, QK_ROPE), F32),
        jax.ShapeDtypeStruct((rows, 2 * D_MODEL), BF16),
    ]
    out_specs = [row_spec(POOL_WIDTH), row_spec(QK_PAD), row_spec(KV_LORA), row_spec(QK_ROPE),
                 row_spec(2 * D_MODEL)]
    if with_kv:
        out_shape += [jax.ShapeDtypeStruct((rows, QK_PAD), BF16)] * 2
        out_specs += [row_spec(QK_PAD)] * 2
    return pl.pallas_call(
        functools.partial(_proj_kernel, sub=min(tm, PROJ_SUB_ROWS)),
        grid=(rows // tm,), in_specs=in_specs, out_specs=out_specs, out_shape=out_shape,
        compiler_params=_params(1), name="proj",
    )(x2d, tabs, gains["g_attn"], wts["w_in"], gains["g_q_lat"], wts["w_uq"], gains["g_kv_lat"],
      gains["gq"], wts["w_uk"], gains["gk"], wts["w_uv"])


def _softmax_step(s, v, m, l, acc):
    m_new = jnp.maximum(m, jnp.max(s, axis=-1, keepdims=True))
    alpha = jnp.exp2(m - m_new)
    p = jnp.exp2(s - m_new)
    l = alpha * l + jnp.sum(p, axis=-1, keepdims=True)
    acc = alpha * acc + _dot(p.astype(BF16), v)
    return m_new, l, acc


def _flash_kernel(q_ref, k_ref, v_ref, km_ref, vm_ref, o_ref, *, tq):
    k_meta, v_meta = km_ref[...], vm_ref[...]
    causal = lax.broadcasted_iota(jnp.int32, (tq, tq), 1) <= lax.broadcasted_iota(jnp.int32, (tq, tq), 0)
    n_q = q_ref.shape[0] // tq
    block = lambda i: slice(i * tq, (i + 1) * tq)
    tasks = [(i, j) for i in range(n_q) for j in range(-1, i + 1)]

    def scores(i, j):
        q = q_ref[block(i), :]
        if j < 0:
            return _dot_nt(q, k_meta)
        s = _dot_nt(q, k_ref[block(j), :])
        return jnp.where(causal, s, -jnp.inf) if j == i else s

    s = scores(*tasks[0])
    for n, (i, j) in enumerate(tasks):
        s_next = scores(*tasks[n + 1]) if n + 1 < len(tasks) else None
        if j < 0:
            carry = (jnp.full((tq, 1), -jnp.inf, F32), jnp.zeros((tq, 1), F32), jnp.zeros((tq, V_HEAD), F32))
        carry = _softmax_step(s, v_meta if j < 0 else v_ref[block(j), :], *carry)
        if j == i:
            o_ref[block(i), :] = (carry[2] / carry[1]).astype(o_ref.dtype)
        s = s_next


def _run_flash(q, k, v, k_meta, v_meta, *, tq):
    b, seq, _ = q.shape
    assert seq % tq == 0
    head_spec = pl.BlockSpec((None, seq, HEAD_PAD), lambda bi, h: (bi, 0, h))
    meta_spec = pl.BlockSpec((N_META, HEAD_PAD), lambda bi, h: (0, h))
    return pl.pallas_call(
        functools.partial(_flash_kernel, tq=tq),
        grid=(b, N_HEADS),
        in_specs=[head_spec, head_spec, head_spec, meta_spec, meta_spec],
        out_specs=head_spec, out_shape=jax.ShapeDtypeStruct((b, seq, N_HEADS * V_HEAD), BF16),
        compiler_params=_params(2), name="flash",
    )(q, k, v, k_meta, v_meta)


def _paged_kernel(pt_ref, q_ref, cn_ref, krn_ref, w_uk_t_ref, w_uv_ref, gk_ref, cache_c_ref, cache_kr_ref,
                  o_ref, c_buf, kr_buf, sem, a_buf, qr_buf, cnew_buf, *, n_pages, n_seq, s_new, tk):
    b = pl.program_id(0)
    slot = b % 2
    rows_hq = N_HEADS * s_new

    def page_copies(seq, sl, j):
        page = pt_ref[seq, j]
        dst = pl.ds(j * PAGE_SIZE, PAGE_SIZE)
        return (pltpu.make_async_copy(cache_c_ref.at[page], c_buf.at[sl, dst, :], sem.at[0, sl]),
                pltpu.make_async_copy(cache_kr_ref.at[page], kr_buf.at[sl, :, dst], sem.at[1, sl]))

    def start_seq(seq, sl):
        for j in range(n_pages):
            for cp in page_copies(seq, sl, j):
                cp.start()

    def wait_seq(seq, sl):
        for j in range(n_pages):
            for cp in page_copies(seq, sl, j):
                cp.wait()

    grp_nope = PAGED_GROUP_HEADS * QK_NOPE
    grp_rows = grp_nope + PAGED_GROUP_HEADS * s_new

    @pl.when(b == 0)
    def _():
        start_seq(0, 0)
        for g in range(N_HEADS // PAGED_GROUP_HEADS):
            a_buf[g * grp_rows:g * grp_rows + grp_nope, :] = w_uk_t_ref[g * grp_nope:(g + 1) * grp_nope, :]
        cnew_buf[...] = jnp.zeros_like(cnew_buf)

    wait_seq(b, slot)

    @pl.when(b + 1 < n_seq)
    def _():
        start_seq(b + 1, 1 - slot)

    qg = q_ref[...] * jnp.concatenate([gk_ref[...]] * N_HEADS, axis=1)
    for h in range(N_HEADS):
        lo = h * HEAD_PAD
        q_nope = qg[:, lo:lo + QK_NOPE].astype(BF16)
        q_abs = _dot(q_nope, w_uk_t_ref[h * QK_NOPE:(h + 1) * QK_NOPE, :])
        g, hh = divmod(h, PAGED_GROUP_HEADS)
        row0 = g * grp_rows + grp_nope + hh * s_new
        a_buf[row0:row0 + s_new, :] = q_abs.astype(BF16)
        qr_buf[h * s_new:(h + 1) * s_new, :] = qg[:, lo + QK_NOPE:lo + QK_HEAD]
    cnew_buf[0:s_new, :] = cn_ref[...]

    a_mats = [a_buf[g * grp_rows:(g + 1) * grp_rows, :] for g in range(N_HEADS // PAGED_GROUP_HEADS)]
    q_rope = qr_buf[...].astype(BF16)

    def tile_matmuls(c_t, kr_t):
        c_bf = c_t.astype(BF16)
        r = [_dot_nt(a, c_bf) for a in a_mats]
        s_rope = _dot(q_rope, kr_t.astype(BF16))
        return c_bf, r, s_rope

    def tile_softmax(mats, kr_t, carry, mask):
        m, l, acc = carry
        c_bf, r, s_rope = mats
        ssq_rope = jnp.sum(kr_t * kr_t, axis=0, keepdims=True)
        parts = []
        for h in range(N_HEADS):
            g, hh = divmod(h, PAGED_GROUP_HEADS)
            kn = r[g][hh * QK_NOPE:(hh + 1) * QK_NOPE, :]
            ssq = jnp.sum(kn * kn, axis=0, keepdims=True) + ssq_rope
            rinv = lax.rsqrt(ssq * (1.0 / QK_HEAD) + EPS)
            s_nope = r[g][grp_nope + hh * s_new:grp_nope + (hh + 1) * s_new, :]
            parts.append((s_nope + s_rope[h * s_new:(h + 1) * s_new, :]) * rinv)
        s = jnp.concatenate(parts, axis=0)
        if mask is not None:
            s = jnp.where(mask, s, -jnp.inf)
        return _softmax_step(s, c_bf, m, l, acc)

    carry = (jnp.full((rows_hq, 1), -jnp.inf, F32), jnp.zeros((rows_hq, 1), F32),
             jnp.zeros((rows_hq, KV_LORA), F32))

    shape = (rows_hq, cnew_buf.shape[0])
    tok = lax.broadcasted_iota(jnp.int32, shape, 0) % s_new
    new_mask = lax.broadcasted_iota(jnp.int32, shape, 1) <= tok
    n_tiles = n_pages * PAGE_SIZE // tk
    c_tile = lambda t: c_buf[slot, t * tk:(t + 1) * tk, :] if t < n_tiles else cnew_buf[...]
    kr_tile = lambda t: kr_buf[slot, :, t * tk:(t + 1) * tk] if t < n_tiles else krn_ref[...]

    mats = tile_matmuls(c_tile(0), kr_tile(0))
    for t in range(n_tiles + 1):
        nxt = tile_matmuls(c_tile(t + 1), kr_tile(t + 1)) if t < n_tiles else None
        carry = tile_softmax(mats, kr_tile(t), carry, None if t < n_tiles else new_mask)
        mats = nxt
    m, l, acc = carry

    ctx = (acc / l).astype(BF16)
    for h in range(N_HEADS):
        o_ref[:, h * V_HEAD:(h + 1) * V_HEAD] = _dot(ctx[h * s_new:(h + 1) * s_new, :],
                                                     w_uv_ref[:, h * V_HEAD:(h + 1) * V_HEAD])


def _run_paged(page_table, q, c_new, kr_new, wts, gains, cache_c, cache_kr, *, s_new, tk):
    n_seq, n_pages = page_table.shape
    past = n_pages * PAGE_SIZE
    assert past % tk == 0 and s_new % 8 == 0 and s_new <= PAGE_SIZE
    rows_hq = N_HEADS * s_new
    seq_spec = lambda w: pl.BlockSpec((s_new, w), lambda b, pt: (b, 0))
    const = lambda shape: pl.BlockSpec(shape, lambda b, pt: (0,) * len(shape), pipeline_mode=pl.Buffered(1))
    grid_spec = pltpu.PrefetchScalarGridSpec(
        num_scalar_prefetch=1, grid=(n_seq,),
        in_specs=[seq_spec(QK_PAD), seq_spec(KV_LORA),
                  pl.BlockSpec((None, QK_ROPE, PAGE_SIZE), lambda b, pt: (b, 0, 0)),
                  const((N_HEADS * QK_NOPE, KV_LORA)), const((KV_LORA, N_HEADS * V_HEAD)), const((1, HEAD_PAD)),
                  pl.BlockSpec(memory_space=pl.ANY), pl.BlockSpec(memory_space=pl.ANY)],
        out_specs=seq_spec(N_HEADS * V_HEAD),
        scratch_shapes=[
            pltpu.VMEM((2, past, KV_LORA), F32),
            pltpu.VMEM((2, QK_ROPE, past), F32),
            pltpu.SemaphoreType.DMA((2, 2)),
            pltpu.VMEM((N_HEADS * QK_NOPE + rows_hq, KV_LORA), BF16),
            pltpu.VMEM((rows_hq, QK_ROPE), F32),
            pltpu.VMEM((PAGE_SIZE, KV_LORA), F32),
        ])
    return pl.pallas_call(
        functools.partial(_paged_kernel, n_pages=n_pages, n_seq=n_seq, s_new=s_new, tk=tk),
        grid_spec=grid_spec, out_shape=jax.ShapeDtypeStruct((n_seq * s_new, N_HEADS * V_HEAD), F32),
        compiler_params=_params(1), name="paged",
    )(page_table, q, c_new, kr_new, wts["w_uk_t"], wts["w_uv"], gains["gk"], cache_c, cache_kr)


def _pool_prompt_kernel(u_ref, halo_ref, meta_ref, d_ref):
    i = pl.program_id(1)
    prev = jnp.where(i == 0, meta_ref[...], halo_ref[...])
    for g, w in enumerate(POOL_WINDOWS):
        cols = slice(g * POOL_GROUP_WIDTH, (g + 1) * POOL_GROUP_WIDTH)
        e = jnp.concatenate([prev[:, cols], u_ref[:, cols]], axis=0)
        win, span = e, 1
        while span < w:
            win = win + pltpu.roll(win, span, 0)
            span *= 2
        d_ref[:, cols] = (win[N_META:, :] * (1.0 / w) - e[N_META:, :]).astype(d_ref.dtype)


def _run_pool_prompt(u3, u_meta, *, tp):
    b, seq, _ = u3.shape
    assert seq % tp == 0 and tp % N_META == 0
    halo_blocks = tp // N_META
    return pl.pallas_call(
        _pool_prompt_kernel, grid=(b, seq // tp),
        in_specs=[pl.BlockSpec((None, tp, POOL_WIDTH), lambda bi, i: (bi, i, 0)),
                  pl.BlockSpec((None, N_META, POOL_WIDTH), lambda bi, i: (bi, jnp.maximum(i * halo_blocks - 1, 0), 0)),
                  pl.BlockSpec((N_META, POOL_WIDTH), lambda bi, i: (0, 0))],
        out_specs=pl.BlockSpec((None, tp, POOL_WIDTH), lambda bi, i: (bi, i, 0)),
        out_shape=jax.ShapeDtypeStruct((b, seq, POOL_WIDTH), BF16),
        compiler_params=_params(2), name="pool_prompt",
    )(u3, u3, u_meta)


def _pool_sample_kernel(e_ref, d_ref, *, s_new):
    for s in range(s_new):
        t = POOL_STATE + s
        for g, w in enumerate(POOL_WINDOWS):
            cols = slice(g * POOL_GROUP_WIDTH, (g + 1) * POOL_GROUP_WIDTH)
            cur = e_ref[t, :, cols]
            win = cur
            for j in range(1, w):
                win = win + e_ref[t - j, :, cols]
            d_ref[s, :, cols] = win * (1.0 / w) - cur


def _run_pool_sample(ext_t, *, s_new):
    t, n_seq, _ = ext_t.shape
    return pl.pallas_call(
        functools.partial(_pool_sample_kernel, s_new=s_new), grid=(1,),
        in_specs=[pl.BlockSpec((t, n_seq, POOL_WIDTH), lambda i: (0, 0, 0))],
        out_specs=pl.BlockSpec((s_new, n_seq, POOL_WIDTH), lambda i: (0, 0, 0)),
        out_shape=jax.ShapeDtypeStruct((s_new, n_seq, POOL_WIDTH), F32),
        compiler_params=_params(1), name="pool_sample",
    )(ext_t)


def _mlp_kernel(x_ref, d_ref, att_ref, gate_ref, w_pool_ref, pool_scale_ref, w_o_ref, g_mlp_ref,
                w_ff1_ref, w_ff2_ref, y_ref, *, ff_chunk):
    d = d_ref[...].astype(BF16)
    pool = jnp.concatenate(
        [_dot(d[:, g * POOL_GROUP_WIDTH:(g + 1) * POOL_GROUP_WIDTH], w_pool_ref[g]) for g in range(POOL_GROUPS)],
        axis=1) * pool_scale_ref[...]
    g_pool = gate_ref[:, :D_MODEL].astype(F32)
    g_att = gate_ref[:, D_MODEL:].astype(F32)
    mix = g_pool * pool + g_att * att_ref[...].astype(F32)
    h = x_ref[...] + _dot(mix.astype(BF16), w_o_ref[...])
    hn = _rms(h, g_mlp_ref[...]).astype(BF16)
    acc = h
    for c0 in range(0, D_FF, ff_chunk):
        z = jnp.maximum(_dot(hn, w_ff1_ref[:, c0:c0 + ff_chunk]), 0.0)
        acc = acc + _dot((z * z).astype(BF16), w_ff2_ref[c0:c0 + ff_chunk, :])
    y_ref[...] = acc


def _run_mlp(x2d, d2d, att2d, gate2d, wts, pool_scale, g_mlp, *, tm, ff_chunk=1024):
    rows = x2d.shape[0]
    assert rows % tm == 0 and D_FF % ff_chunk == 0
    row_spec = lambda w: pl.BlockSpec((tm, w), lambda i: (i, 0))
    return pl.pallas_call(
        functools.partial(_mlp_kernel, ff_chunk=ff_chunk), grid=(rows // tm,),
        in_specs=[row_spec(D_MODEL), row_spec(POOL_WIDTH), row_spec(D_MODEL), row_spec(2 * D_MODEL),
                  _const_spec((POOL_GROUPS, POOL_GROUP_WIDTH, POOL_GROUP_OUT)), _const_spec((1, D_MODEL)),
                  _const_spec((D_MODEL, D_MODEL)), _const_spec((1, D_MODEL)),
                  _const_spec((D_MODEL, D_FF)), _const_spec((D_FF, D_MODEL))],
        out_specs=row_spec(D_MODEL), out_shape=jax.ShapeDtypeStruct((rows, D_MODEL), F32),
        compiler_params=_params(1), name="mlp",
    )(x2d, d2d, att2d, gate2d, wts["w_pool"], pool_scale, wts["w_o"], g_mlp, wts["w_ff1"], wts["w_ff2"])


def _row_tile(rows, target):
    t = min(rows, target)
    while rows % t:
        t //= 2
    return t


def kernel(x_prompt, x_sample, cache_kv_latent, cache_k_rope, state_pool, page_table, meta_tokens, g_attn, w_in,
           g_q_lat, w_uq, g_kv_lat, g_qn_nope, g_qn_rope, w_uk, g_kn_nope, g_kn_rope, w_uv, w_pool_map, pool_scale,
           w_o, g_mlp, w_ff1, w_ff2):
    b, seq, _ = x_prompt.shape
    n_seq, s_new, _ = x_sample.shape
    n_pages = page_table.shape[1]
    past = n_pages * PAGE_SIZE

    wts = _prep_weights(w_in, w_uq, w_uk, w_uv, w_pool_map, w_o, w_ff1, w_ff2)
    row = lambda g: g.reshape(1, -1).astype(F32)
    gains = dict(g_attn=row(g_attn), g_q_lat=row(g_q_lat), g_kv_lat=row(g_kv_lat),
                 gq=_head_gain(g_qn_nope, g_qn_rope) * (SCALE * LOG2_E), gk=_head_gain(g_kn_nope, g_kn_rope))
    pool_scale = row(pool_scale)
    g_mlp = row(g_mlp)

    tm = _row_tile(seq, 512)
    tabs_meta = _rope_tables(jnp.arange(N_META))
    tabs_prompt = _rope_tables(N_META + jnp.arange(seq))
    tm_s = _row_tile(n_seq * s_new, 256)
    tabs_sample = _rope_tables(past + jnp.arange(tm_s) % s_new)

    u_m, _, c_m, kr_m, _, k_m, v_m = _run_proj(meta_tokens.astype(F32), tabs_meta, wts, gains,
                                               tm=N_META, with_kv=True, q_dtype=BF16)

    xp = x_prompt.reshape(b * seq, D_MODEL)
    u_p, q_p, c_p, kr_p, gate_p, k_p, v_p = _run_proj(xp, tabs_prompt, wts, gains, tm=tm, with_kv=True,
                                                      q_dtype=BF16)
    shape3 = lambda a: a.reshape(b, seq, a.shape[-1])
    att_p = _run_flash(shape3(q_p), shape3(k_p), shape3(v_p), k_m, v_m, tq=tm)
    d_p = _run_pool_prompt(shape3(u_p), u_m, tp=tm)
    y_p = _run_mlp(xp, d_p.reshape(b * seq, POOL_WIDTH), att_p.reshape(b * seq, D_MODEL), gate_p, wts,
                   pool_scale, g_mlp, tm=tm)

    xs = x_sample.reshape(n_seq * s_new, D_MODEL)
    u_s, q_s, c_s, kr_s, gate_s = _run_proj(xs, tabs_sample, wts, gains, tm=tm_s, with_kv=False, q_dtype=F32)
    kr_new_t = jnp.pad(kr_s.reshape(n_seq, s_new, QK_ROPE).transpose(0, 2, 1),
                       ((0, 0), (0, 0), (0, PAGE_SIZE - s_new)))
    att_s = _run_paged(page_table, q_s, c_s, kr_new_t, wts, gains, cache_kv_latent,
                       cache_k_rope.transpose(0, 2, 1), s_new=s_new, tk=_row_tile(past, 1024))
    ext_s = jnp.concatenate([state_pool.astype(F32), u_s.reshape(n_seq, s_new, POOL_WIDTH)], axis=1)
    d_s = _run_pool_sample(ext_s.transpose(1, 0, 2), s_new=s_new).transpose(1, 0, 2)
    y_s = _run_mlp(xs, d_s.reshape(n_seq * s_new, POOL_WIDTH), att_s, gate_s, wts, pool_scale, g_mlp, tm=tm_s)

    rep = lambda a: jnp.broadcast_to(a[None], (b,) + a.shape)
    return (
        y_p.reshape(b, seq, D_MODEL),
        y_s.reshape(n_seq, s_new, D_MODEL),
        jnp.concatenate([rep(c_m), shape3(c_p)], axis=1),
        jnp.concatenate([rep(kr_m), shape3(kr_p)], axis=1),
        shape3(u_p)[:, seq - POOL_STATE:],
        c_s.reshape(n_seq, s_new, KV_LORA),
        kr_s.reshape(n_seq, s_new, QK_ROPE),
        ext_s[:, s_new:],
    )
```

```python
import functools

import numpy as np
import jax
import jax.numpy as jnp
from jax import lax
from jax.experimental import pallas as pl
from jax.experimental.pallas import tpu as pltpu

D_MODEL = 1024
N_META = 16
N_HEADS = 8
QK_NOPE = 64
QK_ROPE = 32
ROPE_HALF = QK_ROPE // 2
QK_HEAD = QK_NOPE + QK_ROPE
V_HEAD = D_MODEL // N_HEADS
Q_LORA = 3 * D_MODEL // 8
KV_LORA = D_MODEL // 4
POOL_WIDTH = D_MODEL // 2
POOL_WINDOWS = (2, 4, 8, 16)
POOL_GROUPS = len(POOL_WINDOWS)
POOL_GROUP_WIDTH = POOL_WIDTH // POOL_GROUPS
POOL_GROUP_OUT = D_MODEL // POOL_GROUPS
POOL_STATE = max(POOL_WINDOWS) - 1
D_FF = 4 * D_MODEL
ROPE_BASE = 10000.0
EPS = 1e-6
SCALE = QK_HEAD ** -0.5
LOG2_E = 1.4426950408889634
PAGE_SIZE = 128

LANES = 128
V7X_VMEM_LIMIT_BYTES = 56 * 2 ** 20

HEAD_PAD = LANES
QK_PAD = N_HEADS * HEAD_PAD
KR_PAD = LANES
RAW_Q0, RAW_C0, RAW_KR0 = POOL_WIDTH, POOL_WIDTH + Q_LORA, POOL_WIDTH + Q_LORA + KV_LORA
RAW_G0 = RAW_KR0 + QK_ROPE
IN_U0, IN_Q0, IN_KR0 = 0, POOL_WIDTH, POOL_WIDTH + Q_LORA
IN_C0 = IN_KR0 + KR_PAD
IN_G0 = IN_C0 + KV_LORA
IN_PAD = IN_G0 + 2 * D_MODEL
N_TABS = 4
PROJ_SUB_ROWS = 256
PAGED_GROUP_HEADS = 4
PAGED_KEY_TILE = 2048

F32 = jnp.float32
BF16 = jnp.bfloat16


def _dot(a, b):
    return jnp.dot(a, b, preferred_element_type=F32)


def _dot_nt(a, b):
    return lax.dot_general(a, b, (((1,), (1,)), ((), ())), preferred_element_type=F32)


def _rms(x, g):
    return x * lax.rsqrt(jnp.mean(x * x, axis=-1, keepdims=True) + EPS) * g


def _const_spec(shape):
    zeros = (0,) * len(shape)
    return pl.BlockSpec(shape, lambda *_: zeros, pipeline_mode=pl.Buffered(1))


def _params(n_axes):
    return pltpu.CompilerParams(dimension_semantics=("arbitrary",) * n_axes,
                                vmem_limit_bytes=V7X_VMEM_LIMIT_BYTES)


def _rope_tables(pos):
    pos = np.asarray(pos)
    p = pos.shape[0]
    f32 = np.float32
    inv = (f32(ROPE_BASE) ** (-np.arange(0, QK_ROPE, 2, dtype=f32) / f32(QK_ROPE))).astype(f32)
    ang = pos.astype(f32)[:, None] * inv[None, :]
    cos, sin = np.cos(ang).astype(f32), np.sin(ang).astype(f32)
    z = lambda n: np.zeros((p, n), f32)
    q_cos = np.concatenate([np.ones((p, QK_NOPE), f32), cos, cos, z(HEAD_PAD - QK_HEAD)], axis=1)
    q_sin = np.concatenate([z(QK_NOPE), -sin, sin, z(HEAD_PAD - QK_HEAD)], axis=1)
    half_cos = np.concatenate([cos, cos, z(KR_PAD // 2 - QK_ROPE)], axis=1)
    half_sin = np.concatenate([-sin, sin, z(KR_PAD // 2 - QK_ROPE)], axis=1)
    return jnp.asarray(np.concatenate([q_cos, q_sin, half_cos, half_cos, half_sin, half_sin], axis=1))


def _head_gain(g_nope, g_rope):
    g = jnp.concatenate([g_nope, g_rope, g_rope, jnp.zeros((HEAD_PAD - QK_HEAD,), F32)])
    return g.reshape(1, HEAD_PAD).astype(F32)


def _prep_weights(w_in, w_uq, w_uk, w_uv, w_pool_map, w_o, w_ff1, w_ff2):
    w_kr = w_in[:, RAW_KR0:RAW_G0]
    kr_half = jnp.concatenate([w_kr, w_kr[:, :ROPE_HALF], jnp.zeros((D_MODEL, ROPE_HALF), w_in.dtype)], axis=1)
    w_in_pad = jnp.concatenate([w_in[:, :RAW_C0], kr_half, kr_half, w_in[:, RAW_C0:RAW_KR0], w_in[:, RAW_G0:]],
                               axis=1).astype(BF16)
    uq = w_uq.reshape(Q_LORA, N_HEADS, QK_HEAD)
    uq = jnp.concatenate([uq, uq[:, :, QK_NOPE:QK_NOPE + ROPE_HALF],
                          jnp.zeros((Q_LORA, N_HEADS, HEAD_PAD - QK_HEAD - ROPE_HALF), uq.dtype)], axis=2)
    uq = uq.reshape(Q_LORA, QK_PAD).astype(BF16)
    uk = jnp.pad(w_uk, ((0, 0), (0, 0), (0, HEAD_PAD - QK_NOPE))).reshape(KV_LORA, QK_PAD).astype(BF16)
    uk_t = w_uk.reshape(KV_LORA, N_HEADS * QK_NOPE).T.astype(BF16)
    uv = w_uv.reshape(KV_LORA, N_HEADS * V_HEAD).astype(BF16)
    return dict(w_in=w_in_pad, w_uq=uq, w_uk=uk, w_uk_t=uk_t, w_uv=uv, w_uv_t=uv.T, w_pool=w_pool_map.astype(BF16),
                w_o=w_o.astype(BF16), w_ff1=w_ff1.astype(BF16), w_ff2=w_ff2.astype(BF16))


def _rope_group(x, cos, sin):
    return x * cos + pltpu.roll(x, LANES - ROPE_HALF, 1) * sin


def _norm_head(x, gain):
    ssq = jnp.sum(x * x, axis=-1, keepdims=True)
    return x * lax.rsqrt(ssq * (1.0 / QK_HEAD) + EPS) * gain


def _proj_rows(rows, x_ref, tab_ref, g_attn_ref, w_in_ref, g_q_ref, w_uq_ref, g_kv_ref, gq_ref,
               w_uk_ref, gk_ref, w_uv_ref, u_ref, q_ref, c_ref, kr_ref, gate_ref, kv_refs):
    xn = _rms(x_ref[rows, :], g_attn_ref[...]).astype(BF16)
    u_ref[rows, :] = _dot(xn, w_in_ref[:, IN_U0:IN_Q0])

    c = _rms(_dot(xn, w_in_ref[:, IN_C0:IN_G0]), g_kv_ref[...])
    c_ref[rows, :] = c

    tab = lambda t: tab_ref[rows, t * LANES:(t + 1) * LANES]
    q_kr = _dot(xn, w_in_ref[:, IN_Q0:IN_C0])
    kr = _rope_group(q_kr[:, Q_LORA:], tab(2), tab(3))
    kr_ref[rows, :] = kr[:, :QK_ROPE]

    q_lat = _rms(q_kr[:, :Q_LORA], g_q_ref[...]).astype(BF16)
    q_full = _dot(q_lat, w_uq_ref[...])
    q_gain = gq_ref[...]
    for h in range(N_HEADS):
        sl = slice(h * HEAD_PAD, (h + 1) * HEAD_PAD)
        qh = _rope_group(q_full[:, sl], tab(0), tab(1))
        q_ref[rows, sl] = _norm_head(qh, q_gain).astype(q_ref.dtype)

    if kv_refs:
        k_ref, v_ref = kv_refs
        cb = c.astype(BF16)
        k_full = _dot(cb, w_uk_ref[...])
        lane = lax.broadcasted_iota(jnp.int32, kr.shape, 1)
        kr_hi = jnp.where(lane >= QK_NOPE, kr, 0.0)
        k_gain = gk_ref[...]
        for h in range(N_HEADS):
            sl = slice(h * HEAD_PAD, (h + 1) * HEAD_PAD)
            k_ref[rows, sl] = _norm_head(k_full[:, sl] + kr_hi, k_gain).astype(k_ref.dtype)
        v_ref[:, rows] = _dot_nt(w_uv_ref[...], cb).astype(v_ref.dtype)

    gate = _dot(xn, w_in_ref[:, IN_G0:IN_PAD])
    gate_ref[rows, :] = jax.nn.sigmoid(gate).astype(gate_ref.dtype)


def _proj_kernel(*refs, sub):
    n_in = 11
    ins, outs = refs[:n_in], refs[n_in:]
    tm = ins[0].shape[0]
    for r in range(tm // sub):
        _proj_rows(slice(r * sub, (r + 1) * sub), *ins, *outs[:5], outs[5:])


def _run_proj(x2d, tabs, wts, gains, *, tm, with_kv, q_dtype):
    rows = x2d.shape[0]
    assert rows % tm == 0 and tabs.shape[0] % tm == 0
    tab_blocks = tabs.shape[0] // tm
    row_spec = lambda w: pl.BlockSpec((tm, w), lambda i: (i, 0))
    in_specs = [
        row_spec(D_MODEL),
        pl.BlockSpec((tm, N_TABS * LANES), lambda i: (i % tab_blocks, 0)),
        _const_spec((1, D_MODEL)), _const_spec((D_MODEL, IN_PAD)),
        _const_spec((1, Q_LORA)), _const_spec((Q_LORA, QK_PAD)),
        _const_spec((1, KV_LORA)), _const_spec((1, HEAD_PAD)),
        _const_spec((KV_LORA, QK_PAD)), _const_spec((1, HEAD_PAD)), _const_spec((N_HEADS * V_HEAD, KV_LORA)),
    ]
    out_shape = [
        jax.ShapeDtypeStruct((rows, POOL_WIDTH), F32),
        jax.ShapeDtypeStruct((rows, QK_PAD), q_dtype),
        jax.ShapeDtypeStruct((rows, KV_LORA), F32),
        jax.ShapeDtypeStruct((rows, QK_ROPE), F32),
        jax.ShapeDtypeStruct((rows, 2 * D_MODEL), BF16),
    ]
    out_specs = [row_spec(POOL_WIDTH), row_spec(QK_PAD), row_spec(KV_LORA), row_spec(QK_ROPE),
                 row_spec(2 * D_MODEL)]
    if with_kv:
        out_shape += [jax.ShapeDtypeStruct((rows, QK_PAD), BF16), jax.ShapeDtypeStruct((N_HEADS * V_HEAD, rows), BF16)]
        out_specs += [row_spec(QK_PAD), pl.BlockSpec((N_HEADS * V_HEAD, tm), lambda i: (0, i))]
    return pl.pallas_call(
        functools.partial(_proj_kernel, sub=min(tm, PROJ_SUB_ROWS)),
        grid=(rows // tm,), in_specs=in_specs, out_specs=out_specs, out_shape=out_shape,
        compiler_params=_params(1), name="proj",
    )(x2d, tabs, gains["g_attn"], wts["w_in"], gains["g_q_lat"], wts["w_uq"], gains["g_kv_lat"],
      gains["gq"], wts["w_uk"], gains["gk"], wts["w_uv_t"])


def _softmax_step(s, v, m, l, acc):
    m_new = jnp.maximum(m, jnp.max(s, axis=-1, keepdims=True))
    alpha = jnp.exp2(m - m_new)
    p = jnp.exp2(s - m_new)
    l = alpha * l + jnp.sum(p, axis=-1, keepdims=True)
    acc = alpha * acc + _dot(p.astype(BF16), v)
    return m_new, l, acc


def _softmax_step_t(s_t, v_t, m, l, acc_t):
    m_new = jnp.maximum(m, jnp.max(s_t, axis=0, keepdims=True))
    alpha = jnp.exp2(m - m_new)
    p_t = jnp.exp2(s_t - m_new)
    l = alpha * l + jnp.sum(p_t, axis=0, keepdims=True)
    acc_t = alpha * acc_t + _dot(v_t, p_t.astype(BF16))
    return m_new, l, acc_t


def _flash_kernel(q_ref, k_ref, vt_ref, km_ref, vmt_ref, o_ref, *, tq):
    k_meta, vt_meta = km_ref[...], vmt_ref[...]
    causal = lax.broadcasted_iota(jnp.int32, (tq, tq), 0) <= lax.broadcasted_iota(jnp.int32, (tq, tq), 1)
    n_q = q_ref.shape[0] // tq
    block = lambda i: slice(i * tq, (i + 1) * tq)
    tasks = [(i, j) for i in range(n_q) for j in list(range(i, -1, -1)) + [-1]]

    def scores_t(i, j):
        q = q_ref[block(i), :]
        if j < 0:
            return _dot_nt(k_meta, q)
        s_t = _dot_nt(k_ref[block(j), :], q)
        return jnp.where(causal, s_t, -jnp.inf) if j == i else s_t

    s_t = scores_t(*tasks[0])
    for n, (i, j) in enumerate(tasks):
        s_next = scores_t(*tasks[n + 1]) if n + 1 < len(tasks) else None
        if j == i:
            carry = (jnp.full((1, tq), -jnp.inf, F32), jnp.zeros((1, tq), F32), jnp.zeros((V_HEAD, tq), F32))
        carry = _softmax_step_t(s_t, vt_meta if j < 0 else vt_ref[:, block(j)], *carry)
        if j < 0:
            o_ref[block(i), :] = (carry[2] / carry[1]).T.astype(o_ref.dtype)
        s_t = s_next


def _run_flash(q, k, v_t, k_meta, vt_meta, *, tq):
    b, seq, _ = q.shape
    assert seq % tq == 0
    head_spec = pl.BlockSpec((None, seq, HEAD_PAD), lambda bi, h: (bi, 0, h))
    return pl.pallas_call(
        functools.partial(_flash_kernel, tq=tq),
        grid=(b, N_HEADS),
        in_specs=[head_spec, head_spec, pl.BlockSpec((V_HEAD, seq), lambda bi, h: (h, bi)),
                  pl.BlockSpec((N_META, HEAD_PAD), lambda bi, h: (0, h)),
                  pl.BlockSpec((V_HEAD, N_META), lambda bi, h: (h, 0))],
        out_specs=head_spec, out_shape=jax.ShapeDtypeStruct((b, seq, N_HEADS * V_HEAD), BF16),
        compiler_params=_params(2), name="flash",
    )(q, k, v_t, k_meta, vt_meta)


def _paged_kernel(pt_ref, q_ref, cn_ref, krn_ref, w_uk_t_ref, w_uv_ref, gk_ref, cache_c_ref, cache_kr_ref,
                  o_ref, c_buf, kr_buf, sem, a_buf, qr_buf, cnew_buf, *, n_pages, n_seq, s_new, tk):
    b = pl.program_id(0)
    slot = b % 2
    rows_hq = N_HEADS * s_new

    def page_copies(seq, sl, j):
        page = pt_ref[seq, j]
        dst = pl.ds(j * PAGE_SIZE, PAGE_SIZE)
        return (pltpu.make_async_copy(cache_c_ref.at[page], c_buf.at[sl, dst, :], sem.at[0, sl]),
                pltpu.make_async_copy(cache_kr_ref.at[page], kr_buf.at[sl, :, dst], sem.at[1, sl]))

    def start_seq(seq, sl):
        for j in range(n_pages):
            for cp in page_copies(seq, sl, j):
                cp.start()

    def wait_seq(seq, sl):
        for j in range(n_pages):
            for cp in page_copies(seq, sl, j):
                cp.wait()

    grp_nope = PAGED_GROUP_HEADS * QK_NOPE
    grp_rows = grp_nope + PAGED_GROUP_HEADS * s_new

    @pl.when(b == 0)
    def _():
        start_seq(0, 0)
        for g in range(N_HEADS // PAGED_GROUP_HEADS):
            a_buf[g * grp_rows:g * grp_rows + grp_nope, :] = w_uk_t_ref[g * grp_nope:(g + 1) * grp_nope, :]
        cnew_buf[...] = jnp.zeros_like(cnew_buf)

    wait_seq(b, slot)

    @pl.when(b + 1 < n_seq)
    def _():
        start_seq(b + 1, 1 - slot)

    qg = q_ref[...] * jnp.concatenate([gk_ref[...]] * N_HEADS, axis=1)
    for h in range(N_HEADS):
        lo = h * HEAD_PAD
        q_nope = qg[:, lo:lo + QK_NOPE].astype(BF16)
        q_abs = _dot(q_nope, w_uk_t_ref[h * QK_NOPE:(h + 1) * QK_NOPE, :])
        g, hh = divmod(h, PAGED_GROUP_HEADS)
        row0 = g * grp_rows + grp_nope + hh * s_new
        a_buf[row0:row0 + s_new, :] = q_abs.astype(BF16)
        qr_buf[h * s_new:(h + 1) * s_new, :] = qg[:, lo + QK_NOPE:lo + QK_HEAD]
    cnew_buf[0:s_new, :] = cn_ref[...]

    a_mats = [a_buf[g * grp_rows:(g + 1) * grp_rows, :] for g in range(N_HEADS // PAGED_GROUP_HEADS)]
    q_rope = qr_buf[...].astype(BF16)

    def tile_matmuls(c_t, kr_t):
        c_bf = c_t.astype(BF16)
        r = [_dot_nt(a, c_bf) for a in a_mats]
        s_rope = _dot(q_rope, kr_t.astype(BF16))
        return c_bf, r, s_rope

    def tile_softmax(mats, kr_t, carry, mask):
        m, l, acc = carry
        c_bf, r, s_rope = mats
        ssq_rope = jnp.sum(kr_t * kr_t, axis=0, keepdims=True)
        parts = []
        for h in range(N_HEADS):
            g, hh = divmod(h, PAGED_GROUP_HEADS)
            kn = r[g][hh * QK_NOPE:(hh + 1) * QK_NOPE, :]
            ssq = jnp.sum(kn * kn, axis=0, keepdims=True) + ssq_rope
            rinv = lax.rsqrt(ssq * (1.0 / QK_HEAD) + EPS)
            s_nope = r[g][grp_nope + hh * s_new:grp_nope + (hh + 1) * s_new, :]
            parts.append((s_nope + s_rope[h * s_new:(h + 1) * s_new, :]) * rinv)
        s = jnp.concatenate(parts, axis=0)
        if mask is not None:
            s = jnp.where(mask, s, -jnp.inf)
        return _softmax_step(s, c_bf, m, l, acc)

    carry = (jnp.full((rows_hq, 1), -jnp.inf, F32), jnp.zeros((rows_hq, 1), F32),
             jnp.zeros((rows_hq, KV_LORA), F32))

    shape = (rows_hq, cnew_buf.shape[0])
    tok = lax.broadcasted_iota(jnp.int32, shape, 0) % s_new
    new_mask = lax.broadcasted_iota(jnp.int32, shape, 1) <= tok
    n_tiles = n_pages * PAGE_SIZE // tk
    c_tile = lambda t: c_buf[slot, t * tk:(t + 1) * tk, :] if t < n_tiles else cnew_buf[...]
    kr_tile = lambda t: kr_buf[slot, :, t * tk:(t + 1) * tk] if t < n_tiles else krn_ref[...]

    mats = tile_matmuls(c_tile(0), kr_tile(0))
    for t in range(n_tiles + 1):
        nxt = tile_matmuls(c_tile(t + 1), kr_tile(t + 1)) if t < n_tiles else None
        carry = tile_softmax(mats, kr_tile(t), carry, None if t < n_tiles else new_mask)
        mats = nxt
    m, l, acc = carry

    ctx = (acc / l).astype(BF16)
    for h in range(N_HEADS):
        o_ref[:, h * V_HEAD:(h + 1) * V_HEAD] = _dot(ctx[h * s_new:(h + 1) * s_new, :],
                                                     w_uv_ref[:, h * V_HEAD:(h + 1) * V_HEAD])


def _run_paged(page_table, q, c_new, kr_new, wts, gains, cache_c, cache_kr, *, s_new, tk):
    n_seq, n_pages = page_table.shape
    past = n_pages * PAGE_SIZE
    assert past % tk == 0 and s_new % 8 == 0 and s_new <= PAGE_SIZE
    rows_hq = N_HEADS * s_new
    seq_spec = lambda w: pl.BlockSpec((s_new, w), lambda b, pt: (b, 0))
    const = lambda shape: pl.BlockSpec(shape, lambda b, pt: (0,) * len(shape), pipeline_mode=pl.Buffered(1))
    grid_spec = pltpu.PrefetchScalarGridSpec(
        num_scalar_prefetch=1, grid=(n_seq,),
        in_specs=[seq_spec(QK_PAD), seq_spec(KV_LORA),
                  pl.BlockSpec((None, QK_ROPE, PAGE_SIZE), lambda b, pt: (b, 0, 0)),
                  const((N_HEADS * QK_NOPE, KV_LORA)), const((KV_LORA, N_HEADS * V_HEAD)), const((1, HEAD_PAD)),
                  pl.BlockSpec(memory_space=pl.ANY), pl.BlockSpec(memory_space=pl.ANY)],
        out_specs=seq_spec(N_HEADS * V_HEAD),
        scratch_shapes=[
            pltpu.VMEM((2, past, KV_LORA), F32),
            pltpu.VMEM((2, QK_ROPE, past), F32),
            pltpu.SemaphoreType.DMA((2, 2)),
            pltpu.VMEM((N_HEADS * QK_NOPE + rows_hq, KV_LORA), BF16),
            pltpu.VMEM((rows_hq, QK_ROPE), F32),
            pltpu.VMEM((PAGE_SIZE, KV_LORA), F32),
        ])
    return pl.pallas_call(
        functools.partial(_paged_kernel, n_pages=n_pages, n_seq=n_seq, s_new=s_new, tk=tk),
        grid_spec=grid_spec, out_shape=jax.ShapeDtypeStruct((n_seq * s_new, N_HEADS * V_HEAD), F32),
        compiler_params=_params(1), name="paged",
    )(page_table, q, c_new, kr_new, wts["w_uk_t"], wts["w_uv"], gains["gk"], cache_c, cache_kr)


def _pool_diff(prev, u, w):
    win, span = jnp.concatenate([prev, u], axis=0), 1
    while span < w:
        win = win + pltpu.roll(win, span, 0)
        span *= 2
    return win[N_META:, :] * (1.0 / w) - u


def _pool_sample_kernel(e_ref, d_ref, *, s_new):
    for s in range(s_new):
        t = POOL_STATE + s
        for g, w in enumerate(POOL_WINDOWS):
            cols = slice(g * POOL_GROUP_WIDTH, (g + 1) * POOL_GROUP_WIDTH)
            cur = e_ref[t, :, cols]
            win = cur
            for j in range(1, w):
                win = win + e_ref[t - j, :, cols]
            d_ref[s, :, cols] = win * (1.0 / w) - cur


def _run_pool_sample(ext_t, *, s_new):
    t, n_seq, _ = ext_t.shape
    return pl.pallas_call(
        functools.partial(_pool_sample_kernel, s_new=s_new), grid=(1,),
        in_specs=[pl.BlockSpec((t, n_seq, POOL_WIDTH), lambda i: (0, 0, 0))],
        out_specs=pl.BlockSpec((s_new, n_seq, POOL_WIDTH), lambda i: (0, 0, 0)),
        out_shape=jax.ShapeDtypeStruct((s_new, n_seq, POOL_WIDTH), F32),
        compiler_params=_params(1), name="pool_sample",
    )(ext_t)


def _mlp_kernel(*refs, ff_chunk, pool_from_u):
    n_pool = 3 if pool_from_u else 1
    x_ref, att_ref, gate_ref = refs[:3]
    pool_refs = refs[3:3 + n_pool]
    w_pool_ref, pool_scale_ref, w_o_ref, g_mlp_ref, w_ff1_ref, w_ff2_ref, y_ref = refs[3 + n_pool:]
    group = lambda g: slice(g * POOL_GROUP_WIDTH, (g + 1) * POOL_GROUP_WIDTH)
    if pool_from_u:
        u_ref, halo_ref, meta_ref = pool_refs
        prev = jnp.where(pl.program_id(1) == 0, meta_ref[...], halo_ref[...])
        d = [_pool_diff(prev[:, group(g)], u_ref[:, group(g)], w).astype(BF16) for g, w in enumerate(POOL_WINDOWS)]
    else:
        d = [pool_refs[0][:, group(g)].astype(BF16) for g in range(POOL_GROUPS)]
    pool = jnp.concatenate([_dot(d[g], w_pool_ref[g]) for g in range(POOL_GROUPS)], axis=1) * pool_scale_ref[...]
    g_pool = gate_ref[:, :D_MODEL].astype(F32)
    g_att = gate_ref[:, D_MODEL:].astype(F32)
    mix = g_pool * pool + g_att * att_ref[...].astype(F32)
    h = x_ref[...] + _dot(mix.astype(BF16), w_o_ref[...])
    hn = _rms(h, g_mlp_ref[...]).astype(BF16)
    acc = h
    for c0 in range(0, D_FF, ff_chunk):
        z = jnp.maximum(_dot(hn, w_ff1_ref[:, c0:c0 + ff_chunk]), 0.0)
        acc = acc + _dot((z * z).astype(BF16), w_ff2_ref[c0:c0 + ff_chunk, :])
    y_ref[...] = acc


def _run_mlp(x3, att3, gate3, pool_in, wts, pool_scale, g_mlp, *, tm, ff_chunk=1024):
    n_grp, rows, _ = x3.shape
    assert rows % tm == 0 and D_FF % ff_chunk == 0 and tm % N_META == 0
    row_spec = lambda w: pl.BlockSpec((None, tm, w), lambda g, i: (g, i, 0))
    pool_from_u = len(pool_in) == 2
    if pool_from_u:
        halo_blocks = tm // N_META
        pool_ops = (pool_in[0], pool_in[0], pool_in[1])
        pool_specs = [row_spec(POOL_WIDTH),
                      pl.BlockSpec((None, N_META, POOL_WIDTH),
                                   lambda g, i: (g, jnp.maximum(i * halo_blocks - 1, 0), 0)),
                      pl.BlockSpec((N_META, POOL_WIDTH), lambda g, i: (0, 0))]
    else:
        pool_ops, pool_specs = pool_in, [row_spec(POOL_WIDTH)]
    return pl.pallas_call(
        functools.partial(_mlp_kernel, ff_chunk=ff_chunk, pool_from_u=pool_from_u), grid=(n_grp, rows // tm),
        in_specs=[row_spec(D_MODEL), row_spec(D_MODEL), row_spec(2 * D_MODEL), *pool_specs,
                  _const_spec((POOL_GROUPS, POOL_GROUP_WIDTH, POOL_GROUP_OUT)), _const_spec((1, D_MODEL)),
                  _const_spec((D_MODEL, D_MODEL)), _const_spec((1, D_MODEL)),
                  _const_spec((D_MODEL, D_FF)), _const_spec((D_FF, D_MODEL))],
        out_specs=row_spec(D_MODEL), out_shape=jax.ShapeDtypeStruct((n_grp, rows, D_MODEL), F32),
        compiler_params=_params(2), name="mlp",
    )(x3, att3, gate3, *pool_ops, wts["w_pool"], pool_scale, wts["w_o"], g_mlp, wts["w_ff1"], wts["w_ff2"])


def _row_tile(rows, target):
    t = min(rows, target)
    while rows % t:
        t //= 2
    return t


def kernel(x_prompt, x_sample, cache_kv_latent, cache_k_rope, state_pool, page_table, meta_tokens, g_attn, w_in,
           g_q_lat, w_uq, g_kv_lat, g_qn_nope, g_qn_rope, w_uk, g_kn_nope, g_kn_rope, w_uv, w_pool_map, pool_scale,
           w_o, g_mlp, w_ff1, w_ff2):
    b, seq, _ = x_prompt.shape
    n_seq, s_new, _ = x_sample.shape
    n_pages = page_table.shape[1]
    past = n_pages * PAGE_SIZE

    wts = _prep_weights(w_in, w_uq, w_uk, w_uv, w_pool_map, w_o, w_ff1, w_ff2)
    row = lambda g: g.reshape(1, -1).astype(F32)
    gains = dict(g_attn=row(g_attn), g_q_lat=row(g_q_lat), g_kv_lat=row(g_kv_lat),
                 gq=_head_gain(g_qn_nope, g_qn_rope) * (SCALE * LOG2_E), gk=_head_gain(g_kn_nope, g_kn_rope))
    pool_scale = row(pool_scale)
    g_mlp = row(g_mlp)

    tm = _row_tile(seq, 512)
    tabs_meta = _rope_tables(np.arange(N_META))
    tabs_prompt = _rope_tables(N_META + np.arange(seq))
    tm_s = _row_tile(n_seq * s_new, 256)
    tabs_sample = _rope_tables(past + np.arange(tm_s) % s_new)

    u_m, _, c_m, kr_m, _, k_m, v_m = _run_proj(meta_tokens.astype(F32), tabs_meta, wts, gains,
                                               tm=N_META, with_kv=True, q_dtype=BF16)

    xp = x_prompt.reshape(b * seq, D_MODEL)
    u_p, q_p, c_p, kr_p, gate_p, k_p, v_p = _run_proj(xp, tabs_prompt, wts, gains, tm=tm, with_kv=True,
                                                      q_dtype=BF16)
    shape3 = lambda a: a.reshape(b, seq, a.shape[-1])
    att_p = _run_flash(shape3(q_p), shape3(k_p), v_p, k_m, v_m, tq=tm)
    y_p = _run_mlp(x_prompt, att_p, shape3(gate_p), (shape3(u_p), u_m), wts, pool_scale, g_mlp, tm=tm)

    xs = x_sample.reshape(n_seq * s_new, D_MODEL)
    u_s, q_s, c_s, kr_s, gate_s = _run_proj(xs, tabs_sample, wts, gains, tm=tm_s, with_kv=False, q_dtype=F32)
    kr_new_t = jnp.pad(kr_s.reshape(n_seq, s_new, QK_ROPE).transpose(0, 2, 1),
                       ((0, 0), (0, 0), (0, PAGE_SIZE - s_new)))
    att_s = _run_paged(page_table, q_s, c_s, kr_new_t, wts, gains, cache_kv_latent,
                       cache_k_rope.transpose(0, 2, 1), s_new=s_new, tk=_row_tile(past, PAGED_KEY_TILE))
    ext_s = jnp.concatenate([state_pool.astype(F32), u_s.reshape(n_seq, s_new, POOL_WIDTH)], axis=1)
    d_s = _run_pool_sample(ext_s.transpose(1, 0, 2), s_new=s_new).transpose(1, 0, 2)
    y_s = _run_mlp(xs[None], att_s[None], gate_s[None], (d_s.reshape(1, n_seq * s_new, POOL_WIDTH),), wts,
                   pool_scale, g_mlp, tm=tm_s)

    rep = lambda a: jnp.broadcast_to(a[None], (b,) + a.shape)
    return (
        y_p,
        y_s.reshape(n_seq, s_new, D_MODEL),
        jnp.concatenate([rep(c_m), shape3(c_p)], axis=1),
        jnp.concatenate([rep(kr_m), shape3(kr_p)], axis=1),
        shape3(u_p)[:, seq - POOL_STATE:],
        c_s.reshape(n_seq, s_new, KV_LORA),
        kr_s.reshape(n_seq, s_new, QK_ROPE),
        ext_s[:, s_new:],
    )
```

```python
import functools

import numpy as np
import jax
import jax.numpy as jnp
from jax import lax
from jax.experimental import pallas as pl
from jax.experimental.pallas import tpu as pltpu

D_MODEL = 1024
N_META = 16
N_HEADS = 8
QK_NOPE = 64
QK_ROPE = 32
ROPE_HALF = QK_ROPE // 2
QK_HEAD = QK_NOPE + QK_ROPE
V_HEAD = D_MODEL // N_HEADS
Q_LORA = 3 * D_MODEL // 8
KV_LORA = D_MODEL // 4
POOL_WIDTH = D_MODEL // 2
POOL_WINDOWS = (2, 4, 8, 16)
POOL_GROUPS = len(POOL_WINDOWS)
POOL_GROUP_WIDTH = POOL_WIDTH // POOL_GROUPS
POOL_GROUP_OUT = D_MODEL // POOL_GROUPS
POOL_STATE = max(POOL_WINDOWS) - 1
D_FF = 4 * D_MODEL
ROPE_BASE = 10000.0
EPS = 1e-6
SCALE = QK_HEAD ** -0.5
LOG2_E = 1.4426950408889634
PAGE_SIZE = 128

LANES = 128
V7X_VMEM_LIMIT_BYTES = 56 * 2 ** 20

HEAD_PAD = LANES
QK_PAD = N_HEADS * HEAD_PAD
KR_PAD = LANES
RAW_Q0, RAW_C0, RAW_KR0 = POOL_WIDTH, POOL_WIDTH + Q_LORA, POOL_WIDTH + Q_LORA + KV_LORA
RAW_G0 = RAW_KR0 + QK_ROPE
IN_U0, IN_Q0, IN_KR0 = 0, POOL_WIDTH, POOL_WIDTH + Q_LORA
IN_C0 = IN_KR0 + KR_PAD
IN_G0 = IN_C0 + KV_LORA
IN_PAD = IN_G0 + 2 * D_MODEL
N_TABS = 4
PROJ_SUB_ROWS = 256
PAGED_GROUP_HEADS = 4
PAGED_KEY_TILE = 2048
FLASH_LOOKAHEAD = 3
PAGED_LOOKAHEAD = 2

F32 = jnp.float32
BF16 = jnp.bfloat16


def _dot(a, b):
    return jnp.dot(a, b, preferred_element_type=F32)


def _dot_nt(a, b):
    return lax.dot_general(a, b, (((1,), (1,)), ((), ())), preferred_element_type=F32)


def _rms(x, g):
    return x * lax.rsqrt(jnp.mean(x * x, axis=-1, keepdims=True) + EPS) * g


def _const_spec(shape):
    zeros = (0,) * len(shape)
    return pl.BlockSpec(shape, lambda *_: zeros, pipeline_mode=pl.Buffered(1))


def _params(n_axes):
    return pltpu.CompilerParams(dimension_semantics=("arbitrary",) * n_axes,
                                vmem_limit_bytes=V7X_VMEM_LIMIT_BYTES)


def _rope_tables(pos):
    pos = np.asarray(pos)
    p = pos.shape[0]
    f32 = np.float32
    inv = ROPE_BASE ** (-np.arange(0, QK_ROPE, 2, dtype=np.float64) / QK_ROPE)
    ang = pos.astype(np.float64)[:, None] * inv[None, :]
    cos, sin = np.cos(ang).astype(f32), np.sin(ang).astype(f32)
    z = lambda n: np.zeros((p, n), f32)
    q_cos = np.concatenate([np.ones((p, QK_NOPE), f32), cos, cos, z(HEAD_PAD - QK_HEAD)], axis=1)
    q_sin = np.concatenate([z(QK_NOPE), -sin, sin, z(HEAD_PAD - QK_HEAD)], axis=1)
    half_cos = np.concatenate([cos, cos, z(KR_PAD // 2 - QK_ROPE)], axis=1)
    half_sin = np.concatenate([-sin, sin, z(KR_PAD // 2 - QK_ROPE)], axis=1)
    return jnp.asarray(np.concatenate([q_cos, q_sin, half_cos, half_cos, half_sin, half_sin], axis=1))


def _head_gain(g_nope, g_rope):
    g = jnp.concatenate([g_nope, g_rope, g_rope, jnp.zeros((HEAD_PAD - QK_HEAD,), F32)])
    return g.reshape(1, HEAD_PAD).astype(F32)


def _prep_weights(w_in, w_uq, w_uk, w_uv, w_pool_map, w_o, w_ff1, w_ff2):
    w_kr = w_in[:, RAW_KR0:RAW_G0]
    kr_half = jnp.concatenate([w_kr, w_kr[:, :ROPE_HALF], jnp.zeros((D_MODEL, ROPE_HALF), w_in.dtype)], axis=1)
    w_in_pad = jnp.concatenate([w_in[:, :RAW_C0], kr_half, kr_half, w_in[:, RAW_C0:RAW_KR0], w_in[:, RAW_G0:]],
                               axis=1).astype(BF16)
    uq = w_uq.reshape(Q_LORA, N_HEADS, QK_HEAD)
    uq = jnp.concatenate([uq, uq[:, :, QK_NOPE:QK_NOPE + ROPE_HALF],
                          jnp.zeros((Q_LORA, N_HEADS, HEAD_PAD - QK_HEAD - ROPE_HALF), uq.dtype)], axis=2)
    uq = uq.reshape(Q_LORA, QK_PAD).astype(BF16)
    uk = jnp.pad(w_uk, ((0, 0), (0, 0), (0, HEAD_PAD - QK_NOPE))).reshape(KV_LORA, QK_PAD).astype(BF16)
    uk_t = w_uk.reshape(KV_LORA, N_HEADS * QK_NOPE).T.astype(BF16)
    uv = w_uv.reshape(KV_LORA, N_HEADS * V_HEAD).astype(BF16)
    return dict(w_in=w_in_pad, w_uq=uq, w_uk=uk, w_uk_t=uk_t, w_uv=uv, w_uv_t=uv.T, w_pool=w_pool_map.astype(BF16),
                w_o=w_o.astype(BF16), w_ff1=w_ff1.astype(BF16), w_ff2=w_ff2.astype(BF16))


def _rope_group(x, cos, sin):
    return x * cos + pltpu.roll(x, LANES - ROPE_HALF, 1) * sin


def _norm_head(x, gain):
    ssq = jnp.sum(x * x, axis=-1, keepdims=True)
    return x * lax.rsqrt(ssq * (1.0 / QK_HEAD) + EPS) * gain


def _proj_rows(rows, x_ref, tab_ref, g_attn_ref, w_in_ref, g_q_ref, w_uq_ref, g_kv_ref, gq_ref,
               w_uk_ref, gk_ref, w_uv_ref, u_ref, q_ref, c_ref, kr_ref, gate_ref, kv_refs):
    xn = _rms(x_ref[rows, :], g_attn_ref[...]).astype(BF16)
    u_ref[rows, :] = _dot(xn, w_in_ref[:, IN_U0:IN_Q0])

    c = _rms(_dot(xn, w_in_ref[:, IN_C0:IN_G0]), g_kv_ref[...])
    c_ref[rows, :] = c

    tab = lambda t: tab_ref[rows, t * LANES:(t + 1) * LANES]
    q_kr = _dot(xn, w_in_ref[:, IN_Q0:IN_C0])
    kr = _rope_group(q_kr[:, Q_LORA:], tab(2), tab(3))
    kr_ref[rows, :] = kr[:, :QK_ROPE]

    q_lat = _rms(q_kr[:, :Q_LORA], g_q_ref[...]).astype(BF16)
    q_full = _dot(q_lat, w_uq_ref[...])
    q_gain = gq_ref[...]
    for h in range(N_HEADS):
        sl = slice(h * HEAD_PAD, (h + 1) * HEAD_PAD)
        qh = _rope_group(q_full[:, sl], tab(0), tab(1))
        q_ref[rows, sl] = _norm_head(qh, q_gain).astype(q_ref.dtype)

    if kv_refs:
        k_ref, v_ref = kv_refs
        cb = c.astype(BF16)
        k_full = _dot(cb, w_uk_ref[...])
        lane = lax.broadcasted_iota(jnp.int32, kr.shape, 1)
        kr_hi = jnp.where(lane >= QK_NOPE, kr, 0.0)
        k_gain = gk_ref[...]
        for h in range(N_HEADS):
            sl = slice(h * HEAD_PAD, (h + 1) * HEAD_PAD)
            k_ref[rows, sl] = _norm_head(k_full[:, sl] + kr_hi, k_gain).astype(k_ref.dtype)
        v_ref[:, rows] = _dot_nt(w_uv_ref[...], cb).astype(v_ref.dtype)

    gate = _dot(xn, w_in_ref[:, IN_G0:IN_PAD])
    gate_ref[rows, :] = jax.nn.sigmoid(gate).astype(gate_ref.dtype)


def _proj_kernel(*refs, sub):
    n_in = 11
    ins, outs = refs[:n_in], refs[n_in:]
    tm = ins[0].shape[0]
    for r in range(tm // sub):
        _proj_rows(slice(r * sub, (r + 1) * sub), *ins, *outs[:5], outs[5:])


def _run_proj(x2d, tabs, wts, gains, *, tm, with_kv, q_dtype):
    rows = x2d.shape[0]
    assert rows % tm == 0 and tabs.shape[0] % tm == 0
    tab_blocks = tabs.shape[0] // tm
    row_spec = lambda w: pl.BlockSpec((tm, w), lambda i: (i, 0))
    in_specs = [
        row_spec(D_MODEL),
        pl.BlockSpec((tm, N_TABS * LANES), lambda i: (i % tab_blocks, 0)),
        _const_spec((1, D_MODEL)), _const_spec((D_MODEL, IN_PAD)),
        _const_spec((1, Q_LORA)), _const_spec((Q_LORA, QK_PAD)),
        _const_spec((1, KV_LORA)), _const_spec((1, HEAD_PAD)),
        _const_spec((KV_LORA, QK_PAD)), _const_spec((1, HEAD_PAD)), _const_spec((N_HEADS * V_HEAD, KV_LORA)),
    ]
    out_shape = [
        jax.ShapeDtypeStruct((rows, POOL_WIDTH), F32),
        jax.ShapeDtypeStruct((rows, QK_PAD), q_dtype),
        jax.ShapeDtypeStruct((rows, KV_LORA), F32),
        jax.ShapeDtypeStruct((rows, QK_ROPE), F32),
        jax.ShapeDtypeStruct((rows, 2 * D_MODEL), BF16),
    ]
    out_specs = [row_spec(POOL_WIDTH), row_spec(QK_PAD), row_spec(KV_LORA), row_spec(QK_ROPE),
                 row_spec(2 * D_MODEL)]
    if with_kv:
        out_shape += [jax.ShapeDtypeStruct((rows, QK_PAD), BF16), jax.ShapeDtypeStruct((N_HEADS * V_HEAD, rows), BF16)]
        out_specs += [row_spec(QK_PAD), pl.BlockSpec((N_HEADS * V_HEAD, tm), lambda i: (0, i))]
    return pl.pallas_call(
        functools.partial(_proj_kernel, sub=min(tm, PROJ_SUB_ROWS)),
        grid=(rows // tm,), in_specs=in_specs, out_specs=out_specs, out_shape=out_shape,
        compiler_params=_params(1), name="proj",
    )(x2d, tabs, gains["g_attn"], wts["w_in"], gains["g_q_lat"], wts["w_uq"], gains["g_kv_lat"],
      gains["gq"], wts["w_uk"], gains["gk"], wts["w_uv_t"])


def _softmax_step(s, v, m, l, acc):
    m_new = jnp.maximum(m, jnp.max(s, axis=-1, keepdims=True))
    alpha = jnp.exp2(m - m_new)
    p = jnp.exp2(s - m_new)
    l = alpha * l + jnp.sum(p, axis=-1, keepdims=True)
    acc = alpha * acc + _dot(p.astype(BF16), v)
    return m_new, l, acc


def _softmax_step_t(s_t, v_t, m, l, acc_t):
    m_new = jnp.maximum(m, jnp.max(s_t, axis=0, keepdims=True))
    alpha = jnp.exp2(m - m_new)
    p_t = jnp.exp2(s_t - m_new)
    l = alpha * l + jnp.sum(p_t, axis=0, keepdims=True)
    acc_t = alpha * acc_t + _dot(v_t, p_t.astype(BF16))
    return m_new, l, acc_t


def _flash_kernel(q_ref, k_ref, vt_ref, km_ref, vmt_ref, o_ref, *, tq):
    k_meta, vt_meta = km_ref[...], vmt_ref[...]
    causal = lax.broadcasted_iota(jnp.int32, (tq, tq), 0) <= lax.broadcasted_iota(jnp.int32, (tq, tq), 1)
    n_q = q_ref.shape[0] // tq
    block = lambda i: slice(i * tq, (i + 1) * tq)
    tasks = [(i, j) for i in range(n_q) for j in list(range(i, -1, -1)) + [-1]]

    def scores_t(i, j):
        q = q_ref[block(i), :]
        if j < 0:
            return _dot_nt(k_meta, q)
        s_t = _dot_nt(k_ref[block(j), :], q)
        return jnp.where(causal, s_t, -jnp.inf) if j == i else s_t

    ahead = [scores_t(*t) for t in tasks[:FLASH_LOOKAHEAD]]
    for n, (i, j) in enumerate(tasks):
        if n + FLASH_LOOKAHEAD < len(tasks):
            ahead.append(scores_t(*tasks[n + FLASH_LOOKAHEAD]))
        s_t = ahead.pop(0)
        if j == i:
            carry = (jnp.full((1, tq), -jnp.inf, F32), jnp.zeros((1, tq), F32), jnp.zeros((V_HEAD, tq), F32))
        carry = _softmax_step_t(s_t, vt_meta if j < 0 else vt_ref[:, block(j)], *carry)
        if j < 0:
            o_ref[block(i), :] = (carry[2] / carry[1]).T.astype(o_ref.dtype)


def _run_flash(q, k, v_t, k_meta, vt_meta, *, tq):
    b, seq, _ = q.shape
    assert seq % tq == 0
    head_spec = pl.BlockSpec((None, seq, HEAD_PAD), lambda bi, h: (bi, 0, h))
    return pl.pallas_call(
        functools.partial(_flash_kernel, tq=tq),
        grid=(b, N_HEADS),
        in_specs=[head_spec, head_spec, pl.BlockSpec((V_HEAD, seq), lambda bi, h: (h, bi)),
                  pl.BlockSpec((N_META, HEAD_PAD), lambda bi, h: (0, h)),
                  pl.BlockSpec((V_HEAD, N_META), lambda bi, h: (h, 0))],
        out_specs=head_spec, out_shape=jax.ShapeDtypeStruct((b, seq, N_HEADS * V_HEAD), BF16),
        compiler_params=_params(2), name="flash",
    )(q, k, v_t, k_meta, vt_meta)


def _paged_kernel(pt_ref, q_ref, cn_ref, krn_ref, w_uk_t_ref, w_uv_ref, gk_ref, cache_c_ref, cache_kr_ref,
                  o_ref, c_buf, kr_buf, sem, a_buf, qr_buf, cnew_buf, *, n_pages, n_seq, s_new, tk):
    b = pl.program_id(0)
    slot = b % 2
    rows_hq = N_HEADS * s_new

    def page_copies(seq, sl, j):
        page = pt_ref[seq, j]
        dst = pl.ds(j * PAGE_SIZE, PAGE_SIZE)
        return (pltpu.make_async_copy(cache_c_ref.at[page], c_buf.at[sl, dst, :], sem.at[0, sl]),
                pltpu.make_async_copy(cache_kr_ref.at[page], kr_buf.at[sl, :, dst], sem.at[1, sl]))

    def start_seq(seq, sl):
        for j in range(n_pages):
            for cp in page_copies(seq, sl, j):
                cp.start()

    def wait_seq(sl):
        pltpu.make_async_copy(c_buf.at[sl], c_buf.at[sl], sem.at[0, sl]).wait()
        pltpu.make_async_copy(kr_buf.at[sl], kr_buf.at[sl], sem.at[1, sl]).wait()

    grp_nope = PAGED_GROUP_HEADS * QK_NOPE
    grp_rows = grp_nope + PAGED_GROUP_HEADS * s_new

    @pl.when(b == 0)
    def _():
        start_seq(0, 0)
        for g in range(N_HEADS // PAGED_GROUP_HEADS):
            a_buf[g * grp_rows:g * grp_rows + grp_nope, :] = w_uk_t_ref[g * grp_nope:(g + 1) * grp_nope, :]
        cnew_buf[...] = jnp.zeros_like(cnew_buf)

    qg = q_ref[...] * jnp.concatenate([gk_ref[...]] * N_HEADS, axis=1)
    for h in range(N_HEADS):
        lo = h * HEAD_PAD
        q_nope = qg[:, lo:lo + QK_NOPE].astype(BF16)
        q_abs = _dot(q_nope, w_uk_t_ref[h * QK_NOPE:(h + 1) * QK_NOPE, :])
        g, hh = divmod(h, PAGED_GROUP_HEADS)
        row0 = g * grp_rows + grp_nope + hh * s_new
        a_buf[row0:row0 + s_new, :] = q_abs.astype(BF16)
        qr_buf[h * s_new:(h + 1) * s_new, :] = qg[:, lo + QK_NOPE:lo + QK_HEAD]
    cnew_buf[0:s_new, :] = cn_ref[...]

    wait_seq(slot)

    @pl.when(b + 1 < n_seq)
    def _():
        start_seq(b + 1, 1 - slot)

    a_mats =[a_buf[g * grp_rows:(g + 1) * grp_rows, :] for g in range(N_HEADS // PAGED_GROUP_HEADS)]
    q_rope = qr_buf[...].astype(BF16)

    def tile_matmuls(c_t, kr_t):
        c_bf = c_t.astype(BF16)
        r = [_dot_nt(a, c_bf) for a in a_mats]
        s_rope = _dot(q_rope, kr_t.astype(BF16))
        return c_bf, r, s_rope

    def tile_softmax(mats, kr_t, carry, mask):
        m, l, acc = carry
        c_bf, r, s_rope = mats
        ssq_rope = jnp.sum(kr_t * kr_t, axis=0, keepdims=True)
        parts = []
        for h in range(N_HEADS):
            g, hh = divmod(h, PAGED_GROUP_HEADS)
            kn = r[g][hh * QK_NOPE:(hh + 1) * QK_NOPE, :]
            ssq = jnp.sum(kn * kn, axis=0, keepdims=True) + ssq_rope
            rinv = lax.rsqrt(ssq * (1.0 / QK_HEAD) + EPS)
            s_nope = r[g][grp_nope + hh * s_new:grp_nope + (hh + 1) * s_new, :]
            parts.append((s_nope + s_rope[h * s_new:(h + 1) * s_new, :]) * rinv)
        s = jnp.concatenate(parts, axis=0)
        if mask is not None:
            s = jnp.where(mask, s, -jnp.inf)
        return _softmax_step(s, c_bf, m, l, acc)

    carry = (jnp.full((rows_hq, 1), -jnp.inf, F32), jnp.zeros((rows_hq, 1), F32),
             jnp.zeros((rows_hq, KV_LORA), F32))

    shape = (rows_hq, cnew_buf.shape[0])
    tok = lax.broadcasted_iota(jnp.int32, shape, 0) % s_new
    new_mask = lax.broadcasted_iota(jnp.int32, shape, 1) <= tok
    n_tiles = n_pages * PAGE_SIZE // tk
    c_tile = lambda t: c_buf[slot, t * tk:(t + 1) * tk, :] if t < n_tiles else cnew_buf[...]
    kr_tile = lambda t: kr_buf[slot, :, t * tk:(t + 1) * tk] if t < n_tiles else krn_ref[...]

    ahead = [tile_matmuls(c_tile(t), kr_tile(t)) for t in range(min(PAGED_LOOKAHEAD, n_tiles + 1))]
    for t in range(n_tiles + 1):
        if t + PAGED_LOOKAHEAD <= n_tiles:
            ahead.append(tile_matmuls(c_tile(t + PAGED_LOOKAHEAD), kr_tile(t + PAGED_LOOKAHEAD)))
        carry = tile_softmax(ahead.pop(0), kr_tile(t), carry, None if t < n_tiles else new_mask)
    m, l, acc = carry

    ctx = (acc / l).astype(BF16)
    for h in range(N_HEADS):
        o_ref[:, h * V_HEAD:(h + 1) * V_HEAD] = _dot(ctx[h * s_new:(h + 1) * s_new, :],
                                                     w_uv_ref[:, h * V_HEAD:(h + 1) * V_HEAD])


def _run_paged(page_table, q, c_new, kr_new, wts, gains, cache_c, cache_kr, *, s_new, tk):
    n_seq, n_pages = page_table.shape
    past = n_pages * PAGE_SIZE
    assert past % tk == 0 and s_new % 8 == 0 and s_new <= PAGE_SIZE
    rows_hq = N_HEADS * s_new
    seq_spec = lambda w: pl.BlockSpec((s_new, w), lambda b, pt: (b, 0))
    const = lambda shape: pl.BlockSpec(shape, lambda b, pt: (0,) * len(shape), pipeline_mode=pl.Buffered(1))
    grid_spec = pltpu.PrefetchScalarGridSpec(
        num_scalar_prefetch=1, grid=(n_seq,),
        in_specs=[seq_spec(QK_PAD), seq_spec(KV_LORA),
                  pl.BlockSpec((None, QK_ROPE, PAGE_SIZE), lambda b, pt: (b, 0, 0)),
                  const((N_HEADS * QK_NOPE, KV_LORA)), const((KV_LORA, N_HEADS * V_HEAD)), const((1, HEAD_PAD)),
                  pl.BlockSpec(memory_space=pl.ANY), pl.BlockSpec(memory_space=pl.ANY)],
        out_specs=seq_spec(N_HEADS * V_HEAD),
        scratch_shapes=[
            pltpu.VMEM((2, past, KV_LORA), F32),
            pltpu.VMEM((2, QK_ROPE, past), F32),
            pltpu.SemaphoreType.DMA((2, 2)),
            pltpu.VMEM((N_HEADS * QK_NOPE + rows_hq, KV_LORA), BF16),
            pltpu.VMEM((rows_hq, QK_ROPE), F32),
            pltpu.VMEM((PAGE_SIZE, KV_LORA), F32),
        ])
    return pl.pallas_call(
        functools.partial(_paged_kernel, n_pages=n_pages, n_seq=n_seq, s_new=s_new, tk=tk),
        grid_spec=grid_spec, out_shape=jax.ShapeDtypeStruct((n_seq * s_new, N_HEADS * V_HEAD), F32),
        compiler_params=_params(1), name="paged",
    )(page_table, q, c_new, kr_new, wts["w_uk_t"], wts["w_uv"], gains["gk"], cache_c, cache_kr)


def _pool_diff(prev, u, w):
    win, span = jnp.concatenate([prev, u], axis=0), 1
    while span < w:
        win = win + pltpu.roll(win, span, 0)
        span *= 2
    return win[N_META:, :] * (1.0 / w) - u


def _pool_sample_kernel(e_ref, d_ref, *, s_new):
    for s in range(s_new):
        t = POOL_STATE + s
        for g, w in enumerate(POOL_WINDOWS):
            cols = slice(g * POOL_GROUP_WIDTH, (g + 1) * POOL_GROUP_WIDTH)
            cur = e_ref[t, :, cols]
            win = cur
            for j in range(1, w):
                win = win + e_ref[t - j, :, cols]
            d_ref[s, :, cols] = win * (1.0 / w) - cur


def _run_pool_sample(ext_t, *, s_new):
    t, n_seq, _ = ext_t.shape
    return pl.pallas_call(
        functools.partial(_pool_sample_kernel, s_new=s_new), grid=(1,),
        in_specs=[pl.BlockSpec((t, n_seq, POOL_WIDTH), lambda i: (0, 0, 0))],
        out_specs=pl.BlockSpec((s_new, n_seq, POOL_WIDTH), lambda i: (0, 0, 0)),
        out_shape=jax.ShapeDtypeStruct((s_new, n_seq, POOL_WIDTH), F32),
        compiler_params=_params(1), name="pool_sample",
    )(ext_t)


def _mlp_kernel(*refs, ff_chunk, pool_from_u):
    n_pool = 3 if pool_from_u else 1
    x_ref, att_ref, gate_ref = refs[:3]
    pool_refs = refs[3:3 + n_pool]
    w_pool_ref, pool_scale_ref, w_o_ref, g_mlp_ref, w_ff1_ref, w_ff2_ref, y_ref = refs[3 + n_pool:]
    group = lambda g: slice(g * POOL_GROUP_WIDTH, (g + 1) * POOL_GROUP_WIDTH)
    if pool_from_u:
        u_ref, halo_ref, meta_ref = pool_refs
        prev = jnp.where(pl.program_id(1) == 0, meta_ref[...], halo_ref[...])
        d = [_pool_diff(prev[:, group(g)], u_ref[:, group(g)], w).astype(BF16) for g, w in enumerate(POOL_WINDOWS)]
    else:
        d = [pool_refs[0][:, group(g)].astype(BF16) for g in range(POOL_GROUPS)]
    pool = jnp.concatenate([_dot(d[g], w_pool_ref[g]) for g in range(POOL_GROUPS)], axis=1) * pool_scale_ref[...]
    g_pool = gate_ref[:, :D_MODEL].astype(F32)
    g_att = gate_ref[:, D_MODEL:].astype(F32)
    mix = g_pool * pool + g_att * att_ref[...].astype(F32)
    h = x_ref[...] + _dot(mix.astype(BF16), w_o_ref[...])
    hn = _rms(h, g_mlp_ref[...]).astype(BF16)
    acc = h
    for c0 in range(0, D_FF, ff_chunk):
        z = jnp.maximum(_dot(hn, w_ff1_ref[:, c0:c0 + ff_chunk]), 0.0)
        acc = acc + _dot((z * z).astype(BF16), w_ff2_ref[c0:c0 + ff_chunk, :])
    y_ref[...] = acc


def _run_mlp(x3, att3, gate3, pool_in, wts, pool_scale, g_mlp, *, tm, ff_chunk=1024):
    n_grp, rows, _ = x3.shape
    assert rows % tm == 0 and D_FF % ff_chunk == 0 and tm % N_META == 0
    row_spec = lambda w: pl.BlockSpec((None, tm, w), lambda g, i: (g, i, 0))
    pool_from_u = len(pool_in) == 2
    if pool_from_u:
        halo_blocks = tm // N_META
        pool_ops = (pool_in[0], pool_in[0], pool_in[1])
        pool_specs = [row_spec(POOL_WIDTH),
                      pl.BlockSpec((None, N_META, POOL_WIDTH),
                                   lambda g, i: (g, jnp.maximum(i * halo_blocks - 1, 0), 0)),
                      pl.BlockSpec((N_META, POOL_WIDTH), lambda g, i: (0, 0))]
    else:
        pool_ops, pool_specs = pool_in, [row_spec(POOL_WIDTH)]
    return pl.pallas_call(
        functools.partial(_mlp_kernel, ff_chunk=ff_chunk, pool_from_u=pool_from_u), grid=(n_grp, rows // tm),
        in_specs=[row_spec(D_MODEL), row_spec(D_MODEL), row_spec(2 * D_MODEL), *pool_specs,
                  _const_spec((POOL_GROUPS, POOL_GROUP_WIDTH, POOL_GROUP_OUT)), _const_spec((1, D_MODEL)),
                  _const_spec((D_MODEL, D_MODEL)), _const_spec((1, D_MODEL)),
                  _const_spec((D_MODEL, D_FF)), _const_spec((D_FF, D_MODEL))],
        out_specs=row_spec(D_MODEL), out_shape=jax.ShapeDtypeStruct((n_grp, rows, D_MODEL), F32),
        compiler_params=_params(2), name="mlp",
    )(x3, att3, gate3, *pool_ops, wts["w_pool"], pool_scale, wts["w_o"], g_mlp, wts["w_ff1"], wts["w_ff2"])


def _row_tile(rows, target):
    t = min(rows, target)
    while rows % t:
        t //= 2
    return t


def kernel(x_prompt, x_sample, cache_kv_latent, cache_k_rope, state_pool, page_table, meta_tokens, g_attn, w_in,
           g_q_lat, w_uq, g_kv_lat, g_qn_nope, g_qn_rope, w_uk, g_kn_nope, g_kn_rope, w_uv, w_pool_map, pool_scale,
           w_o, g_mlp, w_ff1, w_ff2):
    b, seq, _ = x_prompt.shape
    n_seq, s_new, _ = x_sample.shape
    n_pages = page_table.shape[1]
    past = n_pages * PAGE_SIZE

    wts = _prep_weights(w_in, w_uq, w_uk, w_uv, w_pool_map, w_o, w_ff1, w_ff2)
    row = lambda g: g.reshape(1, -1).astype(F32)
    gains = dict(g_attn=row(g_attn), g_q_lat=row(g_q_lat), g_kv_lat=row(g_kv_lat),
                 gq=_head_gain(g_qn_nope, g_qn_rope) * (SCALE * LOG2_E), gk=_head_gain(g_kn_nope, g_kn_rope))
    pool_scale = row(pool_scale)
    g_mlp = row(g_mlp)

    tm = _row_tile(seq, 512)
    tabs_meta = _rope_tables(np.arange(N_META))
    tabs_prompt = _rope_tables(N_META + np.arange(seq))
    tm_s = _row_tile(n_seq * s_new, 256)
    tabs_sample = _rope_tables(past + np.arange(tm_s) % s_new)

    u_m, _, c_m, kr_m, _, k_m, v_m = _run_proj(meta_tokens.astype(F32), tabs_meta, wts, gains,
                                               tm=N_META, with_kv=True, q_dtype=BF16)

    xp = x_prompt.reshape(b * seq, D_MODEL)
    u_p, q_p, c_p, kr_p, gate_p, k_p, v_p = _run_proj(xp, tabs_prompt, wts, gains, tm=tm, with_kv=True,
                                                      q_dtype=BF16)
    shape3 = lambda a: a.reshape(b, seq, a.shape[-1])
    att_p = _run_flash(shape3(q_p), shape3(k_p), v_p, k_m, v_m, tq=tm)
    y_p = _run_mlp(x_prompt, att_p, shape3(gate_p), (shape3(u_p), u_m), wts, pool_scale, g_mlp, tm=tm)

    xs = x_sample.reshape(n_seq * s_new, D_MODEL)
    u_s, q_s, c_s, kr_s, gate_s = _run_proj(xs, tabs_sample, wts, gains, tm=tm_s, with_kv=False, q_dtype=F32)
    kr_new_t = jnp.pad(kr_s.reshape(n_seq, s_new, QK_ROPE).transpose(0, 2, 1),
                       ((0, 0), (0, 0), (0, PAGE_SIZE - s_new)))
    att_s = _run_paged(page_table, q_s, c_s, kr_new_t, wts, gains, cache_kv_latent,
                       cache_k_rope.transpose(0, 2, 1), s_new=s_new, tk=_row_tile(past, PAGED_KEY_TILE))
    ext_s = jnp.concatenate([state_pool.astype(F32), u_s.reshape(n_seq, s_new, POOL_WIDTH)], axis=1)
    d_s = _run_pool_sample(ext_s.transpose(1, 0, 2), s_new=s_new).transpose(1, 0, 2)
    y_s = _run_mlp(xs[None], att_s[None], gate_s[None], (d_s.reshape(1, n_seq * s_new, POOL_WIDTH),), wts,
                   pool_scale, g_mlp, tm=tm_s)

    rep = lambda a: jnp.broadcast_to(a[None], (b,) + a.shape)
    return (
        y_p,
        y_s.reshape(n_seq, s_new, D_MODEL),
        jnp.concatenate([rep(c_m), shape3(c_p)], axis=1),
        jnp.concatenate([rep(kr_m), shape3(kr_p)], axis=1),
        shape3(u_p)[:, seq - POOL_STATE:],
        c_s.reshape(n_seq, s_new, KV_LORA),
        kr_s.reshape(n_seq, s_new, QK_ROPE),
        ext_s[:, s_new:],
    )
```

```python
import functools

import numpy as np
import jax
import jax.numpy as jnp
from jax import lax
from jax.experimental import pallas as pl
from jax.experimental.pallas import tpu as pltpu

D_MODEL = 1024
N_META = 16
N_HEADS = 8
QK_NOPE = 64
QK_ROPE = 32
ROPE_HALF = QK_ROPE // 2
QK_HEAD = QK_NOPE + QK_ROPE
V_HEAD = D_MODEL // N_HEADS
Q_LORA = 3 * D_MODEL // 8
KV_LORA = D_MODEL // 4
POOL_WIDTH = D_MODEL // 2
POOL_WINDOWS = (2, 4, 8, 16)
POOL_GROUPS = len(POOL_WINDOWS)
POOL_GROUP_WIDTH = POOL_WIDTH // POOL_GROUPS
POOL_GROUP_OUT = D_MODEL // POOL_GROUPS
POOL_STATE = max(POOL_WINDOWS) - 1
D_FF = 4 * D_MODEL
ROPE_BASE = 10000.0
EPS = 1e-6
SCALE = QK_HEAD ** -0.5
LOG2_E = 1.4426950408889634
PAGE_SIZE = 128

LANES = 128
V7X_VMEM_LIMIT_BYTES = 56 * 2 ** 20

HEAD_PAD = LANES
QK_PAD = N_HEADS * HEAD_PAD
KR_PAD = LANES
RAW_Q0, RAW_C0, RAW_KR0 = POOL_WIDTH, POOL_WIDTH + Q_LORA, POOL_WIDTH + Q_LORA + KV_LORA
RAW_G0 = RAW_KR0 + QK_ROPE
IN_U0, IN_Q0, IN_KR0 = 0, POOL_WIDTH, POOL_WIDTH + Q_LORA
IN_C0 = IN_KR0 + KR_PAD
IN_G0 = IN_C0 + KV_LORA
IN_PAD = IN_G0 + 2 * D_MODEL
N_TABS = 4
PROJ_PROMPT_ROWS = 1024
PROJ_SUB_ROWS = 512
MLP_PROMPT_ROWS = 1024
MLP_SUB_ROWS = 256
PAGED_GROUP_HEADS = 4
PAGED_KEY_TILE = 2048
FLASH_LOOKAHEAD = 3
PAGED_LOOKAHEAD = 2

F32 = jnp.float32
BF16 = jnp.bfloat16


def _dot(a, b):
    return jnp.dot(a, b, preferred_element_type=F32)


def _dot_nt(a, b):
    return lax.dot_general(a, b, (((1,), (1,)), ((), ())), preferred_element_type=F32)


def _rms(x, g):
    return x * lax.rsqrt(jnp.mean(x * x, axis=-1, keepdims=True) + EPS) * g


def _const_spec(shape):
    zeros = (0,) * len(shape)
    return pl.BlockSpec(shape, lambda *_: zeros, pipeline_mode=pl.Buffered(1))


def _params(n_axes):
    return pltpu.CompilerParams(dimension_semantics=("arbitrary",) * n_axes,
                                vmem_limit_bytes=V7X_VMEM_LIMIT_BYTES)


def _rope_tables(pos):
    pos = np.asarray(pos)
    p = pos.shape[0]
    f32 = np.float32
    inv = ROPE_BASE ** (-np.arange(0, QK_ROPE, 2, dtype=np.float64) / QK_ROPE)
    ang = pos.astype(np.float64)[:, None] * inv[None, :]
    cos, sin = np.cos(ang).astype(f32), np.sin(ang).astype(f32)
    z = lambda n: np.zeros((p, n), f32)
    q_cos = np.concatenate([np.ones((p, QK_NOPE), f32), cos, cos, z(HEAD_PAD - QK_HEAD)], axis=1)
    q_sin = np.concatenate([z(QK_NOPE), -sin, sin, z(HEAD_PAD - QK_HEAD)], axis=1)
    half_cos = np.concatenate([cos, cos, z(KR_PAD // 2 - QK_ROPE)], axis=1)
    half_sin = np.concatenate([-sin, sin, z(KR_PAD // 2 - QK_ROPE)], axis=1)
    return jnp.asarray(np.concatenate([q_cos, q_sin, half_cos, half_cos, half_sin, half_sin], axis=1))


def _head_gain(g_nope, g_rope):
    g = jnp.concatenate([g_nope, g_rope, g_rope, jnp.zeros((HEAD_PAD - QK_HEAD,), F32)])
    return g.reshape(1, HEAD_PAD).astype(F32)


def _prep_weights(w_in, w_uq, w_uk, w_uv, w_pool_map, w_o, w_ff1, w_ff2):
    w_kr = w_in[:, RAW_KR0:RAW_G0]
    kr_half = jnp.concatenate([w_kr, w_kr[:, :ROPE_HALF], jnp.zeros((D_MODEL, ROPE_HALF), w_in.dtype)], axis=1)
    w_in_pad = jnp.concatenate([w_in[:, :RAW_C0], kr_half, kr_half, w_in[:, RAW_C0:RAW_KR0], w_in[:, RAW_G0:]],
                               axis=1).astype(BF16)
    uq = w_uq.reshape(Q_LORA, N_HEADS, QK_HEAD)
    uq = jnp.concatenate([uq, uq[:, :, QK_NOPE:QK_NOPE + ROPE_HALF],
                          jnp.zeros((Q_LORA, N_HEADS, HEAD_PAD - QK_HEAD - ROPE_HALF), uq.dtype)], axis=2)
    uq = uq.reshape(Q_LORA, QK_PAD).astype(BF16)
    uk = jnp.pad(w_uk, ((0, 0), (0, 0), (0, HEAD_PAD - QK_NOPE))).reshape(KV_LORA, QK_PAD).astype(BF16)
    uk_t = w_uk.reshape(KV_LORA, N_HEADS * QK_NOPE).T.astype(BF16)
    uv = w_uv.reshape(KV_LORA, N_HEADS * V_HEAD).astype(BF16)
    return dict(w_in=w_in_pad, w_uq=uq, w_uk=uk, w_uk_t=uk_t, w_uv=uv, w_uv_t=uv.T, w_pool=w_pool_map.astype(BF16),
                w_o=w_o.astype(BF16), w_ff1=w_ff1.astype(BF16), w_ff2=w_ff2.astype(BF16))


def _rope_group(x, cos, sin):
    return x * cos + pltpu.roll(x, LANES - ROPE_HALF, 1) * sin


def _norm_head(x, gain):
    ssq = jnp.sum(x * x, axis=-1, keepdims=True)
    return x * lax.rsqrt(ssq * (1.0 / QK_HEAD) + EPS) * gain


def _proj_rows(rows, x_ref, tab_ref, g_attn_ref, w_in_ref, g_q_ref, w_uq_ref, g_kv_ref, gq_ref,
               w_uk_ref, gk_ref, w_uv_ref, u_ref, q_ref, c_ref, kr_ref, gate_ref, kv_refs):
    xn = _rms(x_ref[rows, :], g_attn_ref[...]).astype(BF16)
    u_ref[rows, :] = _dot(xn, w_in_ref[:, IN_U0:IN_Q0])

    c = _rms(_dot(xn, w_in_ref[:, IN_C0:IN_G0]), g_kv_ref[...])
    c_ref[rows, :] = c

    tab = lambda t: tab_ref[rows, t * LANES:(t + 1) * LANES]
    q_kr = _dot(xn, w_in_ref[:, IN_Q0:IN_C0])
    kr = _rope_group(q_kr[:, Q_LORA:], tab(2), tab(3))
    kr_ref[rows, :] = kr[:, :QK_ROPE]

    q_lat = _rms(q_kr[:, :Q_LORA], g_q_ref[...]).astype(BF16)
    q_full = _dot(q_lat, w_uq_ref[...])
    q_gain = gq_ref[...]
    for h in range(N_HEADS):
        sl = slice(h * HEAD_PAD, (h + 1) * HEAD_PAD)
        qh = _rope_group(q_full[:, sl], tab(0), tab(1))
        q_ref[rows, sl] = _norm_head(qh, q_gain).astype(q_ref.dtype)

    if kv_refs:
        k_ref, v_ref = kv_refs
        cb = c.astype(BF16)
        k_full = _dot(cb, w_uk_ref[...])
        lane = lax.broadcasted_iota(jnp.int32, kr.shape, 1)
        kr_hi = jnp.where(lane >= QK_NOPE, kr, 0.0)
        k_gain = gk_ref[...]
        for h in range(N_HEADS):
            sl = slice(h * HEAD_PAD, (h + 1) * HEAD_PAD)
            k_ref[rows, sl] = _norm_head(k_full[:, sl] + kr_hi, k_gain).astype(k_ref.dtype)
        v_ref[:, rows] = _dot_nt(w_uv_ref[...], cb).astype(v_ref.dtype)

    gate = _dot(xn, w_in_ref[:, IN_G0:IN_PAD])
    gate_ref[rows, :] = jax.nn.sigmoid(gate).astype(gate_ref.dtype)


def _proj_kernel(*refs, sub):
    n_in = 11
    ins, outs = refs[:n_in], refs[n_in:]
    tm = ins[0].shape[0]
    for r in range(tm // sub):
        _proj_rows(slice(r * sub, (r + 1) * sub), *ins, *outs[:5], outs[5:])


def _run_proj(x2d, tabs, wts, gains, *, tm, with_kv, q_dtype):
    rows = x2d.shape[0]
    assert rows % tm == 0 and tabs.shape[0] % tm == 0
    tab_blocks = tabs.shape[0] // tm
    row_spec = lambda w: pl.BlockSpec((tm, w), lambda i: (i, 0))
    in_specs = [
        row_spec(D_MODEL),
        pl.BlockSpec((tm, N_TABS * LANES), lambda i: (i % tab_blocks, 0)),
        _const_spec((1, D_MODEL)), _const_spec((D_MODEL, IN_PAD)),
        _const_spec((1, Q_LORA)), _const_spec((Q_LORA, QK_PAD)),
        _const_spec((1, KV_LORA)), _const_spec((1, HEAD_PAD)),
        _const_spec((KV_LORA, QK_PAD)), _const_spec((1, HEAD_PAD)), _const_spec((N_HEADS * V_HEAD, KV_LORA)),
    ]
    out_shape = [
        jax.ShapeDtypeStruct((rows, POOL_WIDTH), F32),
        jax.ShapeDtypeStruct((rows, QK_PAD), q_dtype),
        jax.ShapeDtypeStruct((rows, KV_LORA), F32),
        jax.ShapeDtypeStruct((rows, QK_ROPE), F32),
        jax.ShapeDtypeStruct((rows, 2 * D_MODEL), BF16),
    ]
    out_specs = [row_spec(POOL_WIDTH), row_spec(QK_PAD), row_spec(KV_LORA), row_spec(QK_ROPE),
                 row_spec(2 * D_MODEL)]
    if with_kv:
        out_shape += [jax.ShapeDtypeStruct((rows, QK_PAD), BF16), jax.ShapeDtypeStruct((N_HEADS * V_HEAD, rows), BF16)]
        out_specs += [row_spec(QK_PAD), pl.BlockSpec((N_HEADS * V_HEAD, tm), lambda i: (0, i))]
    return pl.pallas_call(
        functools.partial(_proj_kernel, sub=min(tm, PROJ_SUB_ROWS)),
        grid=(rows // tm,), in_specs=in_specs, out_specs=out_specs, out_shape=out_shape,
        compiler_params=_params(1), name="proj",
    )(x2d, tabs, gains["g_attn"], wts["w_in"], gains["g_q_lat"], wts["w_uq"], gains["g_kv_lat"],
      gains["gq"], wts["w_uk"], gains["gk"], wts["w_uv_t"])


def _softmax_step(s, v, m, l, acc):
    m_new = jnp.maximum(m, jnp.max(s, axis=-1, keepdims=True))
    alpha = jnp.exp2(m - m_new)
    p = jnp.exp2(s - m_new)
    l = alpha * l + jnp.sum(p, axis=-1, keepdims=True)
    acc = alpha * acc + _dot(p.astype(BF16), v)
    return m_new, l, acc


def _softmax_step_t(s_t, v_t, m, l, acc_t):
    m_new = jnp.maximum(m, jnp.max(s_t, axis=0, keepdims=True))
    alpha = jnp.exp2(m - m_new)
    p_t = jnp.exp2(s_t - m_new)
    l = alpha * l + jnp.sum(p_t, axis=0, keepdims=True)
    acc_t = alpha * acc_t + _dot(v_t, p_t.astype(BF16))
    return m_new, l, acc_t


def _flash_kernel(q_ref, k_ref, vt_ref, km_ref, vmt_ref, o_ref, *, tq):
    k_meta, vt_meta = km_ref[...], vmt_ref[...]
    causal = lax.broadcasted_iota(jnp.int32, (tq, tq), 0) <= lax.broadcasted_iota(jnp.int32, (tq, tq), 1)
    n_q = q_ref.shape[0] // tq
    block = lambda i: slice(i * tq, (i + 1) * tq)
    tasks = [(i, j) for i in range(n_q) for j in list(range(i, -1, -1)) + [-1]]

    def scores_t(i, j):
        q = q_ref[block(i), :]
        if j < 0:
            return _dot_nt(k_meta, q)
        s_t = _dot_nt(k_ref[block(j), :], q)
        return jnp.where(causal, s_t, -jnp.inf) if j == i else s_t

    ahead = [scores_t(*t) for t in tasks[:FLASH_LOOKAHEAD]]
    for n, (i, j) in enumerate(tasks):
        if n + FLASH_LOOKAHEAD < len(tasks):
            ahead.append(scores_t(*tasks[n + FLASH_LOOKAHEAD]))
        s_t = ahead.pop(0)
        if j == i:
            carry = (jnp.full((1, tq), -jnp.inf, F32), jnp.zeros((1, tq), F32), jnp.zeros((V_HEAD, tq), F32))
        carry = _softmax_step_t(s_t, vt_meta if j < 0 else vt_ref[:, block(j)], *carry)
        if j < 0:
            o_ref[block(i), :] = (carry[2] / carry[1]).T.astype(o_ref.dtype)


def _run_flash(q, k, v_t, k_meta, vt_meta, *, tq):
    b, seq, _ = q.shape
    assert seq % tq == 0
    head_spec = pl.BlockSpec((None, seq, HEAD_PAD), lambda bi, h: (bi, 0, h))
    return pl.pallas_call(
        functools.partial(_flash_kernel, tq=tq),
        grid=(b, N_HEADS),
        in_specs=[head_spec, head_spec, pl.BlockSpec((V_HEAD, seq), lambda bi, h: (h, bi)),
                  pl.BlockSpec((N_META, HEAD_PAD), lambda bi, h: (0, h)),
                  pl.BlockSpec((V_HEAD, N_META), lambda bi, h: (h, 0))],
        out_specs=head_spec, out_shape=jax.ShapeDtypeStruct((b, seq, N_HEADS * V_HEAD), BF16),
        compiler_params=_params(2), name="flash",
    )(q, k, v_t, k_meta, vt_meta)


def _paged_kernel(pt_ref, q_ref, cn_ref, krn_ref, w_uk_t_ref, w_uv_ref, gk_ref, cache_c_ref, cache_kr_ref,
                  o_ref, c_buf, kr_buf, sem, a_buf, qr_buf, cnew_buf, *, n_pages, n_seq, s_new, tk):
    b = pl.program_id(0)
    slot = b % 2
    rows_hq = N_HEADS * s_new

    def page_copies(seq, sl, j):
        page = pt_ref[seq, j]
        dst = pl.ds(j * PAGE_SIZE, PAGE_SIZE)
        return (pltpu.make_async_copy(cache_c_ref.at[page], c_buf.at[sl, dst, :], sem.at[0, sl]),
                pltpu.make_async_copy(cache_kr_ref.at[page], kr_buf.at[sl, :, dst], sem.at[1, sl]))

    def start_seq(seq, sl):
        for j in range(n_pages):
            for cp in page_copies(seq, sl, j):
                cp.start()

    def wait_seq(sl):
        pltpu.make_async_copy(c_buf.at[sl], c_buf.at[sl], sem.at[0, sl]).wait()
        pltpu.make_async_copy(kr_buf.at[sl], kr_buf.at[sl], sem.at[1, sl]).wait()

    grp_nope = PAGED_GROUP_HEADS * QK_NOPE
    grp_rows = grp_nope + PAGED_GROUP_HEADS * s_new

    @pl.when(b == 0)
    def _():
        start_seq(0, 0)
        for g in range(N_HEADS // PAGED_GROUP_HEADS):
            a_buf[g * grp_rows:g * grp_rows + grp_nope, :] = w_uk_t_ref[g * grp_nope:(g + 1) * grp_nope, :]
        cnew_buf[...] = jnp.zeros_like(cnew_buf)

    qg = q_ref[...] * jnp.concatenate([gk_ref[...]] * N_HEADS, axis=1)
    for h in range(N_HEADS):
        lo = h * HEAD_PAD
        q_nope = qg[:, lo:lo + QK_NOPE].astype(BF16)
        q_abs = _dot(q_nope, w_uk_t_ref[h * QK_NOPE:(h + 1) * QK_NOPE, :])
        g, hh = divmod(h, PAGED_GROUP_HEADS)
        row0 = g * grp_rows + grp_nope + hh * s_new
        a_buf[row0:row0 + s_new, :] = q_abs.astype(BF16)
        qr_buf[h * s_new:(h + 1) * s_new, :] = qg[:, lo + QK_NOPE:lo + QK_HEAD]
    cnew_buf[0:s_new, :] = cn_ref[...]

    wait_seq(slot)

    @pl.when(b + 1 < n_seq)
    def _():
        start_seq(b + 1, 1 - slot)

    a_mats =[a_buf[g * grp_rows:(g + 1) * grp_rows, :] for g in range(N_HEADS // PAGED_GROUP_HEADS)]
    q_rope = qr_buf[...].astype(BF16)

    def tile_matmuls(c_t, kr_t):
        c_bf = c_t.astype(BF16)
        r = [_dot_nt(a, c_bf) for a in a_mats]
        s_rope = _dot(q_rope, kr_t.astype(BF16))
        return c_bf, r, s_rope

    def tile_softmax(mats, kr_t, carry, mask):
        m, l, acc = carry
        c_bf, r, s_rope = mats
        ssq_rope = jnp.sum(kr_t * kr_t, axis=0, keepdims=True)
        parts = []
        for h in range(N_HEADS):
            g, hh = divmod(h, PAGED_GROUP_HEADS)
            kn = r[g][hh * QK_NOPE:(hh + 1) * QK_NOPE, :]
            ssq = jnp.sum(kn * kn, axis=0, keepdims=True) + ssq_rope
            rinv = lax.rsqrt(ssq * (1.0 / QK_HEAD) + EPS)
            s_nope = r[g][grp_nope + hh * s_new:grp_nope + (hh + 1) * s_new, :]
            parts.append((s_nope + s_rope[h * s_new:(h + 1) * s_new, :]) * rinv)
        s = jnp.concatenate(parts, axis=0)
        if mask is not None:
            s = jnp.where(mask, s, -jnp.inf)
        return _softmax_step(s, c_bf, m, l, acc)

    carry = (jnp.full((rows_hq, 1), -jnp.inf, F32), jnp.zeros((rows_hq, 1), F32),
             jnp.zeros((rows_hq, KV_LORA), F32))

    shape = (rows_hq, cnew_buf.shape[0])
    tok = lax.broadcasted_iota(jnp.int32, shape, 0) % s_new
    new_mask = lax.broadcasted_iota(jnp.int32, shape, 1) <= tok
    n_tiles = n_pages * PAGE_SIZE // tk
    c_tile = lambda t: c_buf[slot, t * tk:(t + 1) * tk, :] if t < n_tiles else cnew_buf[...]
    kr_tile = lambda t: kr_buf[slot, :, t * tk:(t + 1) * tk] if t < n_tiles else krn_ref[...]

    ahead = [tile_matmuls(c_tile(t), kr_tile(t)) for t in range(min(PAGED_LOOKAHEAD, n_tiles + 1))]
    for t in range(n_tiles + 1):
        if t + PAGED_LOOKAHEAD <= n_tiles:
            ahead.append(tile_matmuls(c_tile(t + PAGED_LOOKAHEAD), kr_tile(t + PAGED_LOOKAHEAD)))
        carry = tile_softmax(ahead.pop(0), kr_tile(t), carry, None if t < n_tiles else new_mask)
    m, l, acc = carry

    ctx = (acc / l).astype(BF16)
    for h in range(N_HEADS):
        o_ref[:, h * V_HEAD:(h + 1) * V_HEAD] = _dot(ctx[h * s_new:(h + 1) * s_new, :],
                                                     w_uv_ref[:, h * V_HEAD:(h + 1) * V_HEAD])


def _run_paged(page_table, q, c_new, kr_new, wts, gains, cache_c, cache_kr, *, s_new, tk):
    n_seq, n_pages = page_table.shape
    past = n_pages * PAGE_SIZE
    assert past % tk == 0 and s_new % 8 == 0 and s_new <= PAGE_SIZE
    rows_hq = N_HEADS * s_new
    seq_spec = lambda w: pl.BlockSpec((s_new, w), lambda b, pt: (b, 0))
    const = lambda shape: pl.BlockSpec(shape, lambda b, pt: (0,) * len(shape), pipeline_mode=pl.Buffered(1))
    grid_spec = pltpu.PrefetchScalarGridSpec(
        num_scalar_prefetch=1, grid=(n_seq,),
        in_specs=[seq_spec(QK_PAD), seq_spec(KV_LORA),
                  pl.BlockSpec((None, QK_ROPE, PAGE_SIZE), lambda b, pt: (b, 0, 0)),
                  const((N_HEADS * QK_NOPE, KV_LORA)), const((KV_LORA, N_HEADS * V_HEAD)), const((1, HEAD_PAD)),
                  pl.BlockSpec(memory_space=pl.ANY), pl.BlockSpec(memory_space=pl.ANY)],
        out_specs=seq_spec(N_HEADS * V_HEAD),
        scratch_shapes=[
            pltpu.VMEM((2, past, KV_LORA), F32),
            pltpu.VMEM((2, QK_ROPE, past), F32),
            pltpu.SemaphoreType.DMA((2, 2)),
            pltpu.VMEM((N_HEADS * QK_NOPE + rows_hq, KV_LORA), BF16),
            pltpu.VMEM((rows_hq, QK_ROPE), F32),
            pltpu.VMEM((PAGE_SIZE, KV_LORA), F32),
        ])
    return pl.pallas_call(
        functools.partial(_paged_kernel, n_pages=n_pages, n_seq=n_seq, s_new=s_new, tk=tk),
        grid_spec=grid_spec, out_shape=jax.ShapeDtypeStruct((n_seq * s_new, N_HEADS * V_HEAD), F32),
        compiler_params=_params(1), name="paged",
    )(page_table, q, c_new, kr_new, wts["w_uk_t"], wts["w_uv"], gains["gk"], cache_c, cache_kr)


def _pool_diff(prev, u, w):
    win, span = jnp.concatenate([prev, u], axis=0), 1
    while span < w:
        win = win + pltpu.roll(win, span, 0)
        span *= 2
    return win[N_META:, :] * (1.0 / w) - u


def _pool_sample_kernel(e_ref, d_ref, *, s_new):
    for s in range(s_new):
        t = POOL_STATE + s
        for g, w in enumerate(POOL_WINDOWS):
            cols = slice(g * POOL_GROUP_WIDTH, (g + 1) * POOL_GROUP_WIDTH)
            cur = e_ref[t, :, cols]
            win = cur
            for j in range(1, w):
                win = win + e_ref[t - j, :, cols]
            d_ref[s, :, cols] = win * (1.0 / w) - cur


def _run_pool_sample(ext_t, *, s_new):
    t, n_seq, _ = ext_t.shape
    return pl.pallas_call(
        functools.partial(_pool_sample_kernel, s_new=s_new), grid=(1,),
        in_specs=[pl.BlockSpec((t, n_seq, POOL_WIDTH), lambda i: (0, 0, 0))],
        out_specs=pl.BlockSpec((s_new, n_seq, POOL_WIDTH), lambda i: (0, 0, 0)),
        out_shape=jax.ShapeDtypeStruct((s_new, n_seq, POOL_WIDTH), F32),
        compiler_params=_params(1), name="pool_sample",
    )(ext_t)


def _mlp_kernel(*refs, ff_chunk, pool_from_u, sub):
    n_pool = 3 if pool_from_u else 1
    x_ref, att_ref, gate_ref = refs[:3]
    pool_refs = refs[3:3 + n_pool]
    w_pool_ref, pool_scale_ref, w_o_ref, g_mlp_ref, w_ff1_ref, w_ff2_ref, y_ref = refs[3 + n_pool:]
    group = lambda g: slice(g * POOL_GROUP_WIDTH, (g + 1) * POOL_GROUP_WIDTH)

    def merge(rows):
        if pool_from_u:
            u_ref, halo_ref, meta_ref = pool_refs
            if rows.start == 0:
                prev = jnp.where(pl.program_id(1) == 0, meta_ref[...], halo_ref[...])
            else:
                prev = u_ref[rows.start - N_META:rows.start, :]
            d = [_pool_diff(prev[:, group(g)], u_ref[rows, group(g)], w).astype(BF16)
                 for g, w in enumerate(POOL_WINDOWS)]
        else:
            d = [pool_refs[0][rows, group(g)].astype(BF16) for g in range(POOL_GROUPS)]
        pool = jnp.concatenate([_dot(d[g], w_pool_ref[g]) for g in range(POOL_GROUPS)], axis=1)
        g_pool = gate_ref[rows, :D_MODEL].astype(F32)
        g_att = gate_ref[rows, D_MODEL:].astype(F32)
        mix = g_pool * (pool * pool_scale_ref[...]) + g_att * att_ref[rows, :].astype(F32)
        h = x_ref[rows, :] + _dot(mix.astype(BF16), w_o_ref[...])
        return h, _rms(h, g_mlp_ref[...]).astype(BF16)

    def feed_forward(h, hn):
        acc = h
        for c0 in range(0, D_FF, ff_chunk):
            z = jnp.maximum(_dot(hn, w_ff1_ref[:, c0:c0 + ff_chunk]), 0.0)
            acc = acc + _dot((z * z).astype(BF16), w_ff2_ref[c0:c0 + ff_chunk, :])
        return acc

    tm = x_ref.shape[0]
    passes = [slice(r, r + sub) for r in range(0, tm, sub)]
    merged = merge(passes[0])
    for n, rows in enumerate(passes):
        nxt = merge(passes[n + 1]) if n + 1 < len(passes) else None
        y_ref[rows, :] = feed_forward(*merged)
        merged = nxt


def _run_mlp(x3, att3, gate3, pool_in, wts, pool_scale, g_mlp, *, tm, ff_chunk=1024):
    n_grp, rows, _ = x3.shape
    assert rows % tm == 0 and D_FF % ff_chunk == 0 and tm % N_META == 0
    row_spec = lambda w: pl.BlockSpec((None, tm, w), lambda g, i: (g, i, 0))
    pool_from_u = len(pool_in) == 2
    if pool_from_u:
        halo_blocks = tm // N_META
        pool_ops = (pool_in[0], pool_in[0], pool_in[1])
        pool_specs = [row_spec(POOL_WIDTH),
                      pl.BlockSpec((None, N_META, POOL_WIDTH),
                                   lambda g, i: (g, jnp.maximum(i * halo_blocks - 1, 0), 0)),
                      pl.BlockSpec((N_META, POOL_WIDTH), lambda g, i: (0, 0))]
    else:
        pool_ops, pool_specs = pool_in, [row_spec(POOL_WIDTH)]
    return pl.pallas_call(
        functools.partial(_mlp_kernel, ff_chunk=ff_chunk, pool_from_u=pool_from_u, sub=min(tm, MLP_SUB_ROWS)), grid=(n_grp, rows // tm),
        in_specs=[row_spec(D_MODEL), row_spec(D_MODEL), row_spec(2 * D_MODEL), *pool_specs,
                  _const_spec((POOL_GROUPS, POOL_GROUP_WIDTH, POOL_GROUP_OUT)), _const_spec((1, D_MODEL)),
                  _const_spec((D_MODEL, D_MODEL)), _const_spec((1, D_MODEL)),
                  _const_spec((D_MODEL, D_FF)), _const_spec((D_FF, D_MODEL))],
        out_specs=row_spec(D_MODEL), out_shape=jax.ShapeDtypeStruct((n_grp, rows, D_MODEL), F32),
        compiler_params=_params(2), name="mlp",
    )(x3, att3, gate3, *pool_ops, wts["w_pool"], pool_scale, wts["w_o"], g_mlp, wts["w_ff1"], wts["w_ff2"])


def _row_tile(rows, target):
    t = min(rows, target)
    while rows % t:
        t //= 2
    return t


def kernel(x_prompt, x_sample, cache_kv_latent, cache_k_rope, state_pool, page_table, meta_tokens, g_attn, w_in,
           g_q_lat, w_uq, g_kv_lat, g_qn_nope, g_qn_rope, w_uk, g_kn_nope, g_kn_rope, w_uv, w_pool_map, pool_scale,
           w_o, g_mlp, w_ff1, w_ff2):
    b, seq, _ = x_prompt.shape
    n_seq, s_new, _ = x_sample.shape
    n_pages = page_table.shape[1]
    past = n_pages * PAGE_SIZE

    wts = _prep_weights(w_in, w_uq, w_uk, w_uv, w_pool_map, w_o, w_ff1, w_ff2)
    row = lambda g: g.reshape(1, -1).astype(F32)
    gains = dict(g_attn=row(g_attn), g_q_lat=row(g_q_lat), g_kv_lat=row(g_kv_lat),
                 gq=_head_gain(g_qn_nope, g_qn_rope) * (SCALE * LOG2_E), gk=_head_gain(g_kn_nope, g_kn_rope))
    pool_scale = row(pool_scale)
    g_mlp = row(g_mlp)

    tm = _row_tile(seq, 512)
    tabs_meta = _rope_tables(np.arange(N_META))
    tabs_prompt = _rope_tables(N_META + np.arange(seq))
    tm_s = _row_tile(n_seq * s_new, 256)
    tabs_sample = _rope_tables(past + np.arange(tm_s) % s_new)

    u_m, _, c_m, kr_m, _, k_m, v_m = _run_proj(meta_tokens.astype(F32), tabs_meta, wts, gains,
                                               tm=N_META, with_kv=True, q_dtype=BF16)

    xp = x_prompt.reshape(b * seq, D_MODEL)
    u_p, q_p, c_p, kr_p, gate_p, k_p, v_p = _run_proj(xp, tabs_prompt, wts, gains,
                                                      tm=_row_tile(seq, PROJ_PROMPT_ROWS), with_kv=True,
                                                      q_dtype=BF16)
    shape3 = lambda a: a.reshape(b, seq, a.shape[-1])
    att_p = _run_flash(shape3(q_p), shape3(k_p), v_p, k_m, v_m, tq=tm)
    y_p = _run_mlp(x_prompt, att_p, shape3(gate_p), (shape3(u_p), u_m), wts, pool_scale, g_mlp,
                   tm=_row_tile(seq, MLP_PROMPT_ROWS))

    xs = x_sample.reshape(n_seq * s_new, D_MODEL)
    u_s, q_s, c_s, kr_s, gate_s = _run_proj(xs, tabs_sample, wts, gains, tm=tm_s, with_kv=False, q_dtype=F32)
    kr_new_t = jnp.pad(kr_s.reshape(n_seq, s_new, QK_ROPE).transpose(0, 2, 1),
                       ((0, 0), (0, 0), (0, PAGE_SIZE - s_new)))
    att_s = _run_paged(page_table, q_s, c_s, kr_new_t, wts, gains, cache_kv_latent,
                       cache_k_rope.transpose(0, 2, 1), s_new=s_new, tk=_row_tile(past, PAGED_KEY_TILE))
    ext_s = jnp.concatenate([state_pool.astype(F32), u_s.reshape(n_seq, s_new, POOL_WIDTH)], axis=1)
    d_s = _run_pool_sample(ext_s.transpose(1, 0, 2), s_new=s_new).transpose(1, 0, 2)
    y_s = _run_mlp(xs[None], att_s[None], gate_s[None], (d_s.reshape(1, n_seq * s_new, POOL_WIDTH),), wts,
                   pool_scale, g_mlp, tm=tm_s)

    rep = lambda a: jnp.broadcast_to(a[None], (b,) + a.shape)
    return (
        y_p,
        y_s.reshape(n_seq, s_new, D_MODEL),
        jnp.concatenate([rep(c_m), shape3(c_p)], axis=1),
        jnp.concatenate([rep(kr_m), shape3(kr_p)], axis=1),
        shape3(u_p)[:, seq - POOL_STATE:],
        c_s.reshape(n_seq, s_new, KV_LORA),
        kr_s.reshape(n_seq, s_new, QK_ROPE),
        ext_s[:, s_new:],
    )
```

```python
import functools

import numpy as np
import jax
import jax.numpy as jnp
from jax import lax
from jax.experimental import pallas as pl
from jax.experimental.pallas import tpu as pltpu

D_MODEL = 1024
N_META = 16
N_HEADS = 8
QK_NOPE = 64
QK_ROPE = 32
ROPE_HALF = QK_ROPE // 2
QK_HEAD = QK_NOPE + QK_ROPE
V_HEAD = D_MODEL // N_HEADS
Q_LORA = 3 * D_MODEL // 8
KV_LORA = D_MODEL // 4
POOL_WIDTH = D_MODEL // 2
POOL_WINDOWS = (2, 4, 8, 16)
POOL_GROUPS = len(POOL_WINDOWS)
POOL_GROUP_WIDTH = POOL_WIDTH // POOL_GROUPS
POOL_GROUP_OUT = D_MODEL // POOL_GROUPS
POOL_STATE = max(POOL_WINDOWS) - 1
D_FF = 4 * D_MODEL
ROPE_BASE = 10000.0
EPS = 1e-6
SCALE = QK_HEAD ** -0.5
LOG2_E = 1.4426950408889634
PAGE_SIZE = 128

LANES = 128
V7X_VMEM_LIMIT_BYTES = 56 * 2 ** 20

HEAD_PAD = LANES
QK_PAD = N_HEADS * HEAD_PAD
KR_PAD = LANES
RAW_Q0, RAW_C0, RAW_KR0 = POOL_WIDTH, POOL_WIDTH + Q_LORA, POOL_WIDTH + Q_LORA + KV_LORA
RAW_G0 = RAW_KR0 + QK_ROPE
IN_U0, IN_Q0, IN_KR0 = 0, POOL_WIDTH, POOL_WIDTH + Q_LORA
IN_C0 = IN_KR0 + KR_PAD
IN_G0 = IN_C0 + KV_LORA
IN_PAD = IN_G0 + 2 * D_MODEL
N_TABS = 4
PROJ_PROMPT_ROWS = 1024
PROJ_SUB_ROWS = 512
MLP_PROMPT_ROWS = 1024
MLP_SUB_ROWS = 256
PAGED_SEQS = 2
PAGED_GROUP_HEADS = 4
PAGED_KEY_TILE = 2048
FLASH_HEADS = 4
FLASH_LOOKAHEAD = 3
PAGED_LOOKAHEAD = 2

F32 = jnp.float32
BF16 = jnp.bfloat16


def _dot(a, b):
    return jnp.dot(a, b, preferred_element_type=F32)


def _dot_nt(a, b):
    return lax.dot_general(a, b, (((1,), (1,)), ((), ())), preferred_element_type=F32)


def _rms(x, g):
    return x * lax.rsqrt(jnp.mean(x * x, axis=-1, keepdims=True) + EPS) * g


def _const_spec(shape):
    zeros = (0,) * len(shape)
    return pl.BlockSpec(shape, lambda *_: zeros, pipeline_mode=pl.Buffered(1))


def _params(n_axes):
    return pltpu.CompilerParams(dimension_semantics=("arbitrary",) * n_axes,
                                vmem_limit_bytes=V7X_VMEM_LIMIT_BYTES)


def _rope_tables(pos):
    pos = np.asarray(pos)
    p = pos.shape[0]
    f32 = np.float32
    inv = ROPE_BASE ** (-np.arange(0, QK_ROPE, 2, dtype=np.float64) / QK_ROPE)
    ang = pos.astype(np.float64)[:, None] * inv[None, :]
    cos, sin = np.cos(ang).astype(f32), np.sin(ang).astype(f32)
    z = lambda n: np.zeros((p, n), f32)
    q_cos = np.concatenate([np.ones((p, QK_NOPE), f32), cos, cos, z(HEAD_PAD - QK_HEAD)], axis=1)
    q_sin = np.concatenate([z(QK_NOPE), -sin, sin, z(HEAD_PAD - QK_HEAD)], axis=1)
    half_cos = np.concatenate([cos, cos, z(KR_PAD // 2 - QK_ROPE)], axis=1)
    half_sin = np.concatenate([-sin, sin, z(KR_PAD // 2 - QK_ROPE)], axis=1)
    return jnp.asarray(np.concatenate([q_cos, q_sin, half_cos, half_cos, half_sin, half_sin], axis=1))


def _head_gain(g_nope, g_rope):
    g = jnp.concatenate([g_nope, g_rope, g_rope, jnp.zeros((HEAD_PAD - QK_HEAD,), F32)])
    return g.reshape(1, HEAD_PAD).astype(F32)


def _prep_weights(w_in, w_uq, w_uk, w_uv, w_pool_map, w_o, w_ff1, w_ff2):
    w_kr = w_in[:, RAW_KR0:RAW_G0]
    kr_half = jnp.concatenate([w_kr, w_kr[:, :ROPE_HALF], jnp.zeros((D_MODEL, ROPE_HALF), w_in.dtype)], axis=1)
    w_in_pad = jnp.concatenate([w_in[:, :RAW_C0], kr_half, kr_half, w_in[:, RAW_C0:RAW_KR0], w_in[:, RAW_G0:]],
                               axis=1).astype(BF16)
    uq = w_uq.reshape(Q_LORA, N_HEADS, QK_HEAD)
    uq = jnp.concatenate([uq, uq[:, :, QK_NOPE:QK_NOPE + ROPE_HALF],
                          jnp.zeros((Q_LORA, N_HEADS, HEAD_PAD - QK_HEAD - ROPE_HALF), uq.dtype)], axis=2)
    uq = uq.reshape(Q_LORA, QK_PAD).astype(BF16)
    uk = jnp.pad(w_uk, ((0, 0), (0, 0), (0, HEAD_PAD - QK_NOPE))).reshape(KV_LORA, QK_PAD).astype(BF16)
    uk_t = w_uk.reshape(KV_LORA, N_HEADS * QK_NOPE).T.astype(BF16)
    uv = w_uv.reshape(KV_LORA, N_HEADS * V_HEAD).astype(BF16)
    return dict(w_in=w_in_pad, w_uq=uq, w_uk=uk, w_uk_t=uk_t, w_uv=uv, w_uv_t=uv.T, w_pool=w_pool_map.astype(BF16),
                w_o=w_o.astype(BF16), w_ff1=w_ff1.astype(BF16), w_ff2=w_ff2.astype(BF16))


def _rope_group(x, cos, sin):
    return x * cos + pltpu.roll(x, LANES - ROPE_HALF, 1) * sin


def _norm_head(x, gain):
    ssq = jnp.sum(x * x, axis=-1, keepdims=True)
    return x * lax.rsqrt(ssq * (1.0 / QK_HEAD) + EPS) * gain


def _proj_rows(rows, x_ref, tab_ref, g_attn_ref, w_in_ref, g_q_ref, w_uq_ref, g_kv_ref, gq_ref,
               w_uk_ref, gk_ref, w_uv_ref, u_ref, q_ref, c_ref, kr_ref, gate_ref, kv_refs):
    xn = _rms(x_ref[rows, :], g_attn_ref[...]).astype(BF16)
    u_ref[rows, :] = _dot(xn, w_in_ref[:, IN_U0:IN_Q0])

    c = _rms(_dot(xn, w_in_ref[:, IN_C0:IN_G0]), g_kv_ref[...])
    c_ref[rows, :] = c

    tab = lambda t: tab_ref[rows, t * LANES:(t + 1) * LANES]
    q_kr = _dot(xn, w_in_ref[:, IN_Q0:IN_C0])
    kr = _rope_group(q_kr[:, Q_LORA:], tab(2), tab(3))
    kr_ref[rows, :] = kr[:, :QK_ROPE]

    q_lat = _rms(q_kr[:, :Q_LORA], g_q_ref[...]).astype(BF16)
    q_full = _dot(q_lat, w_uq_ref[...])
    q_gain = gq_ref[...]
    for h in range(N_HEADS):
        sl = slice(h * HEAD_PAD, (h + 1) * HEAD_PAD)
        qh = _rope_group(q_full[:, sl], tab(0), tab(1))
        q_ref[rows, sl] = _norm_head(qh, q_gain).astype(q_ref.dtype)

    if kv_refs:
        k_ref, v_ref = kv_refs
        cb = c.astype(BF16)
        k_full = _dot(cb, w_uk_ref[...])
        lane = lax.broadcasted_iota(jnp.int32, kr.shape, 1)
        kr_hi = jnp.where(lane >= QK_NOPE, kr, 0.0)
        k_gain = gk_ref[...]
        for h in range(N_HEADS):
            sl = slice(h * HEAD_PAD, (h + 1) * HEAD_PAD)
            k_ref[rows, sl] = _norm_head(k_full[:, sl] + kr_hi, k_gain).astype(k_ref.dtype)
        v_ref[:, rows] = _dot_nt(w_uv_ref[...], cb).astype(v_ref.dtype)

    gate = _dot(xn, w_in_ref[:, IN_G0:IN_PAD])
    gate_ref[rows, :] = jax.nn.sigmoid(gate).astype(gate_ref.dtype)


def _proj_kernel(*refs, sub):
    n_in = 11
    ins, outs = refs[:n_in], refs[n_in:]
    tm = ins[0].shape[0]
    for r in range(tm // sub):
        _proj_rows(slice(r * sub, (r + 1) * sub), *ins, *outs[:5], outs[5:])


def _run_proj(x2d, tabs, wts, gains, *, tm, with_kv, q_dtype):
    rows = x2d.shape[0]
    assert rows % tm == 0 and tabs.shape[0] % tm == 0
    tab_blocks = tabs.shape[0] // tm
    row_spec = lambda w: pl.BlockSpec((tm, w), lambda i: (i, 0))
    in_specs = [
        row_spec(D_MODEL),
        pl.BlockSpec((tm, N_TABS * LANES), lambda i: (i % tab_blocks, 0)),
        _const_spec((1, D_MODEL)), _const_spec((D_MODEL, IN_PAD)),
        _const_spec((1, Q_LORA)), _const_spec((Q_LORA, QK_PAD)),
        _const_spec((1, KV_LORA)), _const_spec((1, HEAD_PAD)),
        _const_spec((KV_LORA, QK_PAD)), _const_spec((1, HEAD_PAD)), _const_spec((N_HEADS * V_HEAD, KV_LORA)),
    ]
    out_shape = [
        jax.ShapeDtypeStruct((rows, POOL_WIDTH), F32),
        jax.ShapeDtypeStruct((rows, QK_PAD), q_dtype),
        jax.ShapeDtypeStruct((rows, KV_LORA), F32),
        jax.ShapeDtypeStruct((rows, QK_ROPE), F32),
        jax.ShapeDtypeStruct((rows, 2 * D_MODEL), BF16),
    ]
    out_specs = [row_spec(POOL_WIDTH), row_spec(QK_PAD), row_spec(KV_LORA), row_spec(QK_ROPE),
                 row_spec(2 * D_MODEL)]
    if with_kv:
        out_shape += [jax.ShapeDtypeStruct((rows, QK_PAD), BF16), jax.ShapeDtypeStruct((N_HEADS * V_HEAD, rows), BF16)]
        out_specs += [row_spec(QK_PAD), pl.BlockSpec((N_HEADS * V_HEAD, tm), lambda i: (0, i))]
    return pl.pallas_call(
        functools.partial(_proj_kernel, sub=min(tm, PROJ_SUB_ROWS)),
        grid=(rows // tm,), in_specs=in_specs, out_specs=out_specs, out_shape=out_shape,
        compiler_params=_params(1), name="proj",
    )(x2d, tabs, gains["g_attn"], wts["w_in"], gains["g_q_lat"], wts["w_uq"], gains["g_kv_lat"],
      gains["gq"], wts["w_uk"], gains["gk"], wts["w_uv_t"])


def _softmax_step(s, v, m, l, acc):
    m_new = jnp.maximum(m, jnp.max(s, axis=-1, keepdims=True))
    alpha = jnp.exp2(m - m_new)
    p = jnp.exp2(s - m_new)
    l = alpha * l + jnp.sum(p, axis=-1, keepdims=True)
    acc = alpha * acc + _dot(p.astype(BF16), v)
    return m_new, l, acc


def _softmax_step_t(s_t, v_t, m, l, acc_t):
    m_new = jnp.maximum(m, jnp.max(s_t, axis=0, keepdims=True))
    alpha = jnp.exp2(m - m_new)
    p_t = jnp.exp2(s_t - m_new)
    l = alpha * l + jnp.sum(p_t, axis=0, keepdims=True)
    acc_t = alpha * acc_t + _dot(v_t, p_t.astype(BF16))
    return m_new, l, acc_t


def _flash_kernel(q_ref, k_ref, vt_ref, km_ref, vmt_ref, o_ref, *, tq):
    causal = lax.broadcasted_iota(jnp.int32, (tq, tq), 0) <= lax.broadcasted_iota(jnp.int32, (tq, tq), 1)
    n_q = q_ref.shape[0] // tq
    block = lambda i: slice(i * tq, (i + 1) * tq)
    lanes = lambda h: slice(h * HEAD_PAD, (h + 1) * HEAD_PAD)
    tasks = [(h, i, j) for h in range(FLASH_HEADS) for i in range(n_q) for j in list(range(i, -1, -1)) + [-1]]

    def scores_t(h, i, j):
        q = q_ref[block(i), lanes(h)]
        if j < 0:
            return _dot_nt(km_ref[:, lanes(h)], q)
        s_t = _dot_nt(k_ref[block(j), lanes(h)], q)
        return jnp.where(causal, s_t, -jnp.inf) if j == i else s_t

    def values_t(h, j):
        rows = slice(h * V_HEAD, (h + 1) * V_HEAD)
        return vmt_ref[rows, :] if j < 0 else vt_ref[rows, block(j)]

    ahead = [scores_t(*t) for t in tasks[:FLASH_LOOKAHEAD]]
    for n, (h, i, j) in enumerate(tasks):
        if n + FLASH_LOOKAHEAD < len(tasks):
            ahead.append(scores_t(*tasks[n + FLASH_LOOKAHEAD]))
        s_t = ahead.pop(0)
        if j == i:
            carry = (jnp.full((1, tq), -jnp.inf, F32), jnp.zeros((1, tq), F32), jnp.zeros((V_HEAD, tq), F32))
        carry = _softmax_step_t(s_t, values_t(h, j), *carry)
        if j < 0:
            o_ref[block(i), lanes(h)] = (carry[2] / carry[1]).T.astype(o_ref.dtype)


def _run_flash(q, k, v_t, k_meta, vt_meta, *, tq):
    b, seq, _ = q.shape
    assert seq % tq == 0 and N_HEADS % FLASH_HEADS == 0
    width = FLASH_HEADS * HEAD_PAD
    head_spec = pl.BlockSpec((None, seq, width), lambda bi, h: (bi, 0, h))
    return pl.pallas_call(
        functools.partial(_flash_kernel, tq=tq),
        grid=(b, N_HEADS // FLASH_HEADS),
        in_specs=[head_spec, head_spec, pl.BlockSpec((FLASH_HEADS * V_HEAD, seq), lambda bi, h: (h, bi)),
                  pl.BlockSpec((N_META, width), lambda bi, h: (0, h)),
                  pl.BlockSpec((FLASH_HEADS * V_HEAD, N_META), lambda bi, h: (h, 0))],
        out_specs=head_spec, out_shape=jax.ShapeDtypeStruct((b, seq, N_HEADS * V_HEAD), BF16),
        compiler_params=_params(2), name="flash",
    )(q, k, v_t, k_meta, vt_meta)


def _paged_kernel(pt_ref, q_ref, cn_ref, krn_ref, w_uk_t_ref, w_uv_ref, gk_ref, cache_c_ref, cache_kr_ref,
                  o_ref, c_buf, kr_buf, sem, a_buf, qr_buf, cnew_buf, *, n_pages, n_seq, s_new, tk):
    step = pl.program_id(0)
    seq0 = step * PAGED_SEQS
    rows_hq = N_HEADS * s_new

    def page_copies(seq, sl, j):
        page = pt_ref[seq, j]
        dst = pl.ds(j * PAGE_SIZE, PAGE_SIZE)
        return (pltpu.make_async_copy(cache_c_ref.at[page], c_buf.at[sl, dst, :], sem.at[0, sl]),
                pltpu.make_async_copy(cache_kr_ref.at[page], kr_buf.at[sl, :, dst], sem.at[1, sl]))

    def start_seq(seq, sl):
        for j in range(n_pages):
            for cp in page_copies(seq, sl, j):
                cp.start()

    def wait_seq(sl):
        pltpu.make_async_copy(c_buf.at[sl], c_buf.at[sl], sem.at[0, sl]).wait()
        pltpu.make_async_copy(kr_buf.at[sl], kr_buf.at[sl], sem.at[1, sl]).wait()

    n_grp = N_HEADS // PAGED_GROUP_HEADS
    grp_nope = PAGED_GROUP_HEADS * QK_NOPE
    grp_rows = grp_nope + PAGED_GROUP_HEADS * s_new

    @pl.when(step == 0)
    def _():
        start_seq(0, 0)
        for p in range(PAGED_SEQS):
            for g in range(n_grp):
                a_buf[p, g * grp_rows:g * grp_rows + grp_nope, :] = w_uk_t_ref[g * grp_nope:(g + 1) * grp_nope, :]
        cnew_buf[...] = jnp.zeros_like(cnew_buf)

    def prepare(p):
        rows = slice(p * s_new, (p + 1) * s_new)
        qg = q_ref[rows, :] * jnp.concatenate([gk_ref[...]] * N_HEADS, axis=1)
        for h in range(N_HEADS):
            lo = h * HEAD_PAD
            q_nope = qg[:, lo:lo + QK_NOPE].astype(BF16)
            q_abs = _dot(q_nope, w_uk_t_ref[h * QK_NOPE:(h + 1) * QK_NOPE, :])
            g, hh = divmod(h, PAGED_GROUP_HEADS)
            row0 = g * grp_rows + grp_nope + hh * s_new
            a_buf[p, row0:row0 + s_new, :] = q_abs.astype(BF16)
            qr_buf[p, h * s_new:(h + 1) * s_new, :] = qg[:, lo + QK_NOPE:lo + QK_HEAD]
        cnew_buf[p, 0:s_new, :] = cn_ref[rows, :]

    shape = (rows_hq, PAGE_SIZE)
    tok = lax.broadcasted_iota(jnp.int32, shape, 0) % s_new
    new_mask = lax.broadcasted_iota(jnp.int32, shape, 1) <= tok
    n_tiles = n_pages * PAGE_SIZE // tk

    def attend(p):
        a_mats = [a_buf[p, g * grp_rows:(g + 1) * grp_rows, :] for g in range(n_grp)]
        q_rope = qr_buf[p].astype(BF16)

        def tile_matmuls(c_t, kr_t):
            c_bf = c_t.astype(BF16)
            r = [_dot_nt(a, c_bf) for a in a_mats]
            s_rope = _dot(q_rope, kr_t.astype(BF16))
            return c_bf, r, s_rope

        def tile_softmax(mats, kr_t, carry, mask):
            m, l, acc = carry
            c_bf, r, s_rope = mats
            ssq_rope = jnp.sum(kr_t * kr_t, axis=0, keepdims=True)
            parts = []
            for h in range(N_HEADS):
                g, hh = divmod(h, PAGED_GROUP_HEADS)
                kn = r[g][hh * QK_NOPE:(hh + 1) * QK_NOPE, :]
                ssq = jnp.sum(kn * kn, axis=0, keepdims=True) + ssq_rope
                rinv = lax.rsqrt(ssq * (1.0 / QK_HEAD) + EPS)
                s_nope = r[g][grp_nope + hh * s_new:grp_nope + (hh + 1) * s_new, :]
                parts.append((s_nope + s_rope[h * s_new:(h + 1) * s_new, :]) * rinv)
            s = jnp.concatenate(parts, axis=0)
            if mask is not None:
                s = jnp.where(mask, s, -jnp.inf)
            return _softmax_step(s, c_bf, m, l, acc)

        c_tile = lambda t: c_buf[p, t * tk:(t + 1) * tk, :] if t < n_tiles else cnew_buf[p]
        kr_tile = lambda t: kr_buf[p, :, t * tk:(t + 1) * tk] if t < n_tiles else krn_ref[p]
        carry = (jnp.full((rows_hq, 1), -jnp.inf, F32), jnp.zeros((rows_hq, 1), F32),
                 jnp.zeros((rows_hq, KV_LORA), F32))
        ahead = [tile_matmuls(c_tile(t), kr_tile(t)) for t in range(min(PAGED_LOOKAHEAD, n_tiles + 1))]
        for t in range(n_tiles + 1):
            if t + PAGED_LOOKAHEAD <= n_tiles:
                ahead.append(tile_matmuls(c_tile(t + PAGED_LOOKAHEAD), kr_tile(t + PAGED_LOOKAHEAD)))
            carry = tile_softmax(ahead.pop(0), kr_tile(t), carry, None if t < n_tiles else new_mask)
        m, l, acc = carry

        ctx = (acc / l).astype(BF16)
        for h in range(N_HEADS):
            o_ref[p * s_new:(p + 1) * s_new, h * V_HEAD:(h + 1) * V_HEAD] = _dot(
                ctx[h * s_new:(h + 1) * s_new, :], w_uv_ref[:, h * V_HEAD:(h + 1) * V_HEAD])

    for p in range(PAGED_SEQS):
        prepare(p)
        wait_seq(p)
        if p + 1 < PAGED_SEQS:
            start_seq(seq0 + p + 1, p + 1)
        else:
            @pl.when(seq0 + PAGED_SEQS < n_seq)
            def _():
                start_seq(seq0 + PAGED_SEQS, 0)
        attend(p)


def _run_paged(page_table, q, c_new, kr_new, wts, gains, cache_c, cache_kr, *, s_new, tk):
    n_seq, n_pages = page_table.shape
    past = n_pages * PAGE_SIZE
    assert past % tk == 0 and s_new % 8 == 0 and s_new <= PAGE_SIZE and n_seq % PAGED_SEQS == 0
    rows_hq = N_HEADS * s_new
    seq_spec = lambda w: pl.BlockSpec((PAGED_SEQS * s_new, w), lambda b, pt: (b, 0))
    const = lambda shape: pl.BlockSpec(shape, lambda b, pt: (0,) * len(shape), pipeline_mode=pl.Buffered(1))
    grid_spec = pltpu.PrefetchScalarGridSpec(
        num_scalar_prefetch=1, grid=(n_seq // PAGED_SEQS,),
        in_specs=[seq_spec(QK_PAD), seq_spec(KV_LORA),
                  pl.BlockSpec((PAGED_SEQS, QK_ROPE, PAGE_SIZE), lambda b, pt: (b, 0, 0)),
                  const((N_HEADS * QK_NOPE, KV_LORA)), const((KV_LORA, N_HEADS * V_HEAD)), const((1, HEAD_PAD)),
                  pl.BlockSpec(memory_space=pl.ANY), pl.BlockSpec(memory_space=pl.ANY)],
        out_specs=seq_spec(N_HEADS * V_HEAD),
        scratch_shapes=[
            pltpu.VMEM((PAGED_SEQS, past, KV_LORA), F32),
            pltpu.VMEM((PAGED_SEQS, QK_ROPE, past), F32),
            pltpu.SemaphoreType.DMA((2, PAGED_SEQS)),
            pltpu.VMEM((PAGED_SEQS, N_HEADS * QK_NOPE + rows_hq, KV_LORA), BF16),
            pltpu.VMEM((PAGED_SEQS, rows_hq, QK_ROPE), F32),
            pltpu.VMEM((PAGED_SEQS, PAGE_SIZE, KV_LORA), F32),
        ])
    return pl.pallas_call(
        functools.partial(_paged_kernel, n_pages=n_pages, n_seq=n_seq, s_new=s_new, tk=tk),
        grid_spec=grid_spec, out_shape=jax.ShapeDtypeStruct((n_seq * s_new, N_HEADS * V_HEAD), F32),
        compiler_params=_params(1), name="paged",
    )(page_table, q, c_new, kr_new, wts["w_uk_t"], wts["w_uv"], gains["gk"], cache_c, cache_kr)


def _pool_diff(prev, u, w):
    win, span = jnp.concatenate([prev, u], axis=0), 1
    while span < w:
        win = win + pltpu.roll(win, span, 0)
        span *= 2
    return win[N_META:, :] * (1.0 / w) - u


def _pool_sample_kernel(e_ref, d_ref, *, s_new):
    for s in range(s_new):
        t = POOL_STATE + s
        for g, w in enumerate(POOL_WINDOWS):
            cols = slice(g * POOL_GROUP_WIDTH, (g + 1) * POOL_GROUP_WIDTH)
            cur = e_ref[t, :, cols]
            win = cur
            for j in range(1, w):
                win = win + e_ref[t - j, :, cols]
            d_ref[s, :, cols] = win * (1.0 / w) - cur


def _run_pool_sample(ext_t, *, s_new):
    t, n_seq, _ = ext_t.shape
    return pl.pallas_call(
        functools.partial(_pool_sample_kernel, s_new=s_new), grid=(1,),
        in_specs=[pl.BlockSpec((t, n_seq, POOL_WIDTH), lambda i: (0, 0, 0))],
        out_specs=pl.BlockSpec((s_new, n_seq, POOL_WIDTH), lambda i: (0, 0, 0)),
        out_shape=jax.ShapeDtypeStruct((s_new, n_seq, POOL_WIDTH), F32),
        compiler_params=_params(1), name="pool_sample",
    )(ext_t)


def _mlp_kernel(*refs, ff_chunk, pool_from_u, sub):
    n_pool = 3 if pool_from_u else 1
    x_ref, att_ref, gate_ref = refs[:3]
    pool_refs = refs[3:3 + n_pool]
    w_pool_ref, pool_scale_ref, w_o_ref, g_mlp_ref, w_ff1_ref, w_ff2_ref, y_ref = refs[3 + n_pool:]
    group = lambda g: slice(g * POOL_GROUP_WIDTH, (g + 1) * POOL_GROUP_WIDTH)

    def merge(rows):
        if pool_from_u:
            u_ref, halo_ref, meta_ref = pool_refs
            if rows.start == 0:
                prev = jnp.where(pl.program_id(1) == 0, meta_ref[...], halo_ref[...])
            else:
                prev = u_ref[rows.start - N_META:rows.start, :]
            d = [_pool_diff(prev[:, group(g)], u_ref[rows, group(g)], w).astype(BF16)
                 for g, w in enumerate(POOL_WINDOWS)]
        else:
            d = [pool_refs[0][rows, group(g)].astype(BF16) for g in range(POOL_GROUPS)]
        pool = jnp.concatenate([_dot(d[g], w_pool_ref[g]) for g in range(POOL_GROUPS)], axis=1)
        g_pool = gate_ref[rows, :D_MODEL].astype(F32)
        g_att = gate_ref[rows, D_MODEL:].astype(F32)
        mix = g_pool * (pool * pool_scale_ref[...]) + g_att * att_ref[rows, :].astype(F32)
        h = x_ref[rows, :] + _dot(mix.astype(BF16), w_o_ref[...])
        return h, _rms(h, g_mlp_ref[...]).astype(BF16)

    def feed_forward(h, hn):
        acc = h
        for c0 in range(0, D_FF, ff_chunk):
            z = jnp.maximum(_dot(hn, w_ff1_ref[:, c0:c0 + ff_chunk]), 0.0)
            acc = acc + _dot((z * z).astype(BF16), w_ff2_ref[c0:c0 + ff_chunk, :])
        return acc

    tm = x_ref.shape[0]
    passes = [slice(r, r + sub) for r in range(0, tm, sub)]
    merged = merge(passes[0])
    for n, rows in enumerate(passes):
        nxt = merge(passes[n + 1]) if n + 1 < len(passes) else None
        y_ref[rows, :] = feed_forward(*merged)
        merged = nxt


def _run_mlp(x3, att3, gate3, pool_in, wts, pool_scale, g_mlp, *, tm, ff_chunk=1024):
    n_grp, rows, _ = x3.shape
    assert rows % tm == 0 and D_FF % ff_chunk == 0 and tm % N_META == 0
    row_spec = lambda w: pl.BlockSpec((None, tm, w), lambda g, i: (g, i, 0))
    pool_from_u = len(pool_in) == 2
    if pool_from_u:
        halo_blocks = tm // N_META
        pool_ops = (pool_in[0], pool_in[0], pool_in[1])
        pool_specs = [row_spec(POOL_WIDTH),
                      pl.BlockSpec((None, N_META, POOL_WIDTH),
                                   lambda g, i: (g, jnp.maximum(i * halo_blocks - 1, 0), 0)),
                      pl.BlockSpec((N_META, POOL_WIDTH), lambda g, i: (0, 0))]
    else:
        pool_ops, pool_specs = pool_in, [row_spec(POOL_WIDTH)]
    return pl.pallas_call(
        functools.partial(_mlp_kernel, ff_chunk=ff_chunk, pool_from_u=pool_from_u, sub=min(tm, MLP_SUB_ROWS)), grid=(n_grp, rows // tm),
        in_specs=[row_spec(D_MODEL), row_spec(D_MODEL), row_spec(2 * D_MODEL), *pool_specs,
                  _const_spec((POOL_GROUPS, POOL_GROUP_WIDTH, POOL_GROUP_OUT)), _const_spec((1, D_MODEL)),
                  _const_spec((D_MODEL, D_MODEL)), _const_spec((1, D_MODEL)),
                  _const_spec((D_MODEL, D_FF)), _const_spec((D_FF, D_MODEL))],
        out_specs=row_spec(D_MODEL), out_shape=jax.ShapeDtypeStruct((n_grp, rows, D_MODEL), F32),
        compiler_params=_params(2), name="mlp",
    )(x3, att3, gate3, *pool_ops, wts["w_pool"], pool_scale, wts["w_o"], g_mlp, wts["w_ff1"], wts["w_ff2"])


def _row_tile(rows, target):
    t = min(rows, target)
    while rows % t:
        t //= 2
    return t


def kernel(x_prompt, x_sample, cache_kv_latent, cache_k_rope, state_pool, page_table, meta_tokens, g_attn, w_in,
           g_q_lat, w_uq, g_kv_lat, g_qn_nope, g_qn_rope, w_uk, g_kn_nope, g_kn_rope, w_uv, w_pool_map, pool_scale,
           w_o, g_mlp, w_ff1, w_ff2):
    b, seq, _ = x_prompt.shape
    n_seq, s_new, _ = x_sample.shape
    n_pages = page_table.shape[1]
    past = n_pages * PAGE_SIZE

    wts = _prep_weights(w_in, w_uq, w_uk, w_uv, w_pool_map, w_o, w_ff1, w_ff2)
    row = lambda g: g.reshape(1, -1).astype(F32)
    gains = dict(g_attn=row(g_attn), g_q_lat=row(g_q_lat), g_kv_lat=row(g_kv_lat),
                 gq=_head_gain(g_qn_nope, g_qn_rope) * (SCALE * LOG2_E), gk=_head_gain(g_kn_nope, g_kn_rope))
    pool_scale = row(pool_scale)
    g_mlp = row(g_mlp)

    tm = _row_tile(seq, 512)
    tabs_meta = _rope_tables(np.arange(N_META))
    tabs_prompt = _rope_tables(N_META + np.arange(seq))
    tm_s = _row_tile(n_seq * s_new, 256)
    tabs_sample = _rope_tables(past + np.arange(tm_s) % s_new)

    u_m, _, c_m, kr_m, _, k_m, v_m = _run_proj(meta_tokens.astype(F32), tabs_meta, wts, gains,
                                               tm=N_META, with_kv=True, q_dtype=BF16)

    xp = x_prompt.reshape(b * seq, D_MODEL)
    u_p, q_p, c_p, kr_p, gate_p, k_p, v_p = _run_proj(xp, tabs_prompt, wts, gains,
                                                      tm=_row_tile(seq, PROJ_PROMPT_ROWS), with_kv=True,
                                                      q_dtype=BF16)
    shape3 = lambda a: a.reshape(b, seq, a.shape[-1])
    att_p = _run_flash(shape3(q_p), shape3(k_p), v_p, k_m, v_m, tq=tm)
    y_p = _run_mlp(x_prompt, att_p, shape3(gate_p), (shape3(u_p), u_m), wts, pool_scale, g_mlp,
                   tm=_row_tile(seq, MLP_PROMPT_ROWS))

    xs = x_sample.reshape(n_seq * s_new, D_MODEL)
    u_s, q_s, c_s, kr_s, gate_s = _run_proj(xs, tabs_sample, wts, gains, tm=tm_s, with_kv=False, q_dtype=F32)
    kr_new_t = jnp.pad(kr_s.reshape(n_seq, s_new, QK_ROPE).transpose(0, 2, 1),
                       ((0, 0), (0, 0), (0, PAGE_SIZE - s_new)))
    att_s = _run_paged(page_table, q_s, c_s, kr_new_t, wts, gains, cache_kv_latent,
                       cache_k_rope.transpose(0, 2, 1), s_new=s_new, tk=_row_tile(past, PAGED_KEY_TILE))
    ext_s = jnp.concatenate([state_pool.astype(F32), u_s.reshape(n_seq, s_new, POOL_WIDTH)], axis=1)
    d_s = _run_pool_sample(ext_s.transpose(1, 0, 2), s_new=s_new).transpose(1, 0, 2)
    y_s = _run_mlp(xs[None], att_s[None], gate_s[None], (d_s.reshape(1, n_seq * s_new, POOL_WIDTH),), wts,
                   pool_scale, g_mlp, tm=tm_s)

    rep = lambda a: jnp.broadcast_to(a[None], (b,) + a.shape)
    return (
        y_p,
        y_s.reshape(n_seq, s_new, D_MODEL),
        jnp.concatenate([rep(c_m), shape3(c_p)], axis=1),
        jnp.concatenate([rep(kr_m), shape3(kr_p)], axis=1),
        shape3(u_p)[:, seq - POOL_STATE:],
        c_s.reshape(n_seq, s_new, KV_LORA),
        kr_s.reshape(n_seq, s_new, QK_ROPE),
        ext_s[:, s_new:],
    )
```

```python
import functools

import numpy as np
import jax
import jax.numpy as jnp
from jax import lax
from jax.experimental import pallas as pl
from jax.experimental.pallas import tpu as pltpu

D_MODEL = 1024
N_META = 16
N_HEADS = 8
QK_NOPE = 64
QK_ROPE = 32
ROPE_HALF = QK_ROPE // 2
QK_HEAD = QK_NOPE + QK_ROPE
V_HEAD = D_MODEL // N_HEADS
Q_LORA = 3 * D_MODEL // 8
KV_LORA = D_MODEL // 4
POOL_WIDTH = D_MODEL // 2
POOL_WINDOWS = (2, 4, 8, 16)
POOL_GROUPS = len(POOL_WINDOWS)
POOL_GROUP_WIDTH = POOL_WIDTH // POOL_GROUPS
POOL_GROUP_OUT = D_MODEL // POOL_GROUPS
POOL_STATE = max(POOL_WINDOWS) - 1
D_FF = 4 * D_MODEL
ROPE_BASE = 10000.0
EPS = 1e-6
SCALE = QK_HEAD ** -0.5
LOG2_E = 1.4426950408889634
PAGE_SIZE = 128

LANES = 128
V7X_VMEM_LIMIT_BYTES = 56 * 2 ** 20

HEAD_PAD = LANES
QK_PAD = N_HEADS * HEAD_PAD
KR_PAD = LANES
RAW_Q0, RAW_C0, RAW_KR0 = POOL_WIDTH, POOL_WIDTH + Q_LORA, POOL_WIDTH + Q_LORA + KV_LORA
RAW_G0 = RAW_KR0 + QK_ROPE
IN_U0, IN_Q0, IN_KR0 = 0, POOL_WIDTH, POOL_WIDTH + Q_LORA
IN_C0 = IN_KR0 + KR_PAD
IN_G0 = IN_C0 + KV_LORA
IN_PAD = IN_G0 + 2 * D_MODEL
N_TABS = 4
PROJ_PROMPT_ROWS = 1024
PROJ_SUB_ROWS = 512
MLP_PROMPT_ROWS = 1024
MLP_SUB_ROWS = 256
PAGED_SEQS = 2
PAGED_GROUP_HEADS = 8
PAGED_KEY_TILE = 2048
FLASH_HEADS = 4
FLASH_LOOKAHEAD = 1
PAGED_LOOKAHEAD = 2

F32 = jnp.float32
BF16 = jnp.bfloat16


def _dot(a, b):
    return jnp.dot(a, b, preferred_element_type=F32)


def _dot_nt(a, b):
    return lax.dot_general(a, b, (((1,), (1,)), ((), ())), preferred_element_type=F32)


def _rms(x, g):
    return x * lax.rsqrt(jnp.mean(x * x, axis=-1, keepdims=True) + EPS) * g


def _const_spec(shape):
    zeros = (0,) * len(shape)
    return pl.BlockSpec(shape, lambda *_: zeros, pipeline_mode=pl.Buffered(1))


def _params(n_axes):
    return pltpu.CompilerParams(dimension_semantics=("arbitrary",) * n_axes,
                                vmem_limit_bytes=V7X_VMEM_LIMIT_BYTES)


def _rope_tables(pos):
    pos = np.asarray(pos)
    p = pos.shape[0]
    f32 = np.float32
    inv = ROPE_BASE ** (-np.arange(0, QK_ROPE, 2, dtype=np.float64) / QK_ROPE)
    ang = pos.astype(np.float64)[:, None] * inv[None, :]
    cos, sin = np.cos(ang).astype(f32), np.sin(ang).astype(f32)
    z = lambda n: np.zeros((p, n), f32)
    q_cos = np.concatenate([np.ones((p, QK_NOPE), f32), cos, cos, z(HEAD_PAD - QK_HEAD)], axis=1)
    q_sin = np.concatenate([z(QK_NOPE), -sin, sin, z(HEAD_PAD - QK_HEAD)], axis=1)
    half_cos = np.concatenate([cos, cos, z(KR_PAD // 2 - QK_ROPE)], axis=1)
    half_sin = np.concatenate([-sin, sin, z(KR_PAD // 2 - QK_ROPE)], axis=1)
    return jnp.asarray(np.concatenate([q_cos, q_sin, half_cos, half_cos, half_sin, half_sin], axis=1))


def _head_gain(g_nope, g_rope):
    g = jnp.concatenate([g_nope, g_rope, g_rope, jnp.zeros((HEAD_PAD - QK_HEAD,), F32)])
    return g.reshape(1, HEAD_PAD).astype(F32)


def _prep_weights(w_in, w_uq, w_uk, w_uv, w_pool_map, w_o, w_ff1, w_ff2):
    w_kr = w_in[:, RAW_KR0:RAW_G0]
    kr_half = jnp.concatenate([w_kr, w_kr[:, :ROPE_HALF], jnp.zeros((D_MODEL, ROPE_HALF), w_in.dtype)], axis=1)
    w_in_pad = jnp.concatenate([w_in[:, :RAW_C0], kr_half, kr_half, w_in[:, RAW_C0:RAW_KR0], w_in[:, RAW_G0:]],
                               axis=1).astype(BF16)
    uq = w_uq.reshape(Q_LORA, N_HEADS, QK_HEAD)
    uq = jnp.concatenate([uq, uq[:, :, QK_NOPE:QK_NOPE + ROPE_HALF],
                          jnp.zeros((Q_LORA, N_HEADS, HEAD_PAD - QK_HEAD - ROPE_HALF), uq.dtype)], axis=2)
    uq = uq.reshape(Q_LORA, QK_PAD).astype(BF16)
    uk = jnp.pad(w_uk, ((0, 0), (0, 0), (0, HEAD_PAD - QK_NOPE))).reshape(KV_LORA, QK_PAD).astype(BF16)
    uk_t = w_uk.reshape(KV_LORA, N_HEADS * QK_NOPE).T.astype(BF16)
    uv = w_uv.reshape(KV_LORA, N_HEADS * V_HEAD).astype(BF16)
    return dict(w_in=w_in_pad, w_uq=uq, w_uk=uk, w_uk_t=uk_t, w_uv=uv, w_uv_t=uv.T, w_pool=w_pool_map.astype(BF16),
                w_o=w_o.astype(BF16), w_ff1=w_ff1.astype(BF16), w_ff2=w_ff2.astype(BF16))


def _rope_group(x, cos, sin):
    return x * cos + pltpu.roll(x, LANES - ROPE_HALF, 1) * sin


def _norm_head(x, gain):
    ssq = jnp.sum(x * x, axis=-1, keepdims=True)
    return x * lax.rsqrt(ssq * (1.0 / QK_HEAD) + EPS) * gain


def _proj_rows(rows, x_ref, tab_ref, g_attn_ref, w_in_ref, g_q_ref, w_uq_ref, g_kv_ref, gq_ref,
               w_uk_ref, gk_ref, w_uv_ref, u_ref, q_ref, c_ref, kr_ref, gate_ref, kv_refs):
    xn = _rms(x_ref[rows, :], g_attn_ref[...]).astype(BF16)
    u_ref[rows, :] = _dot(xn, w_in_ref[:, IN_U0:IN_Q0])

    c = _rms(_dot(xn, w_in_ref[:, IN_C0:IN_G0]), g_kv_ref[...])
    c_ref[rows, :] = c

    tab = lambda t: tab_ref[rows, t * LANES:(t + 1) * LANES]
    q_kr = _dot(xn, w_in_ref[:, IN_Q0:IN_C0])
    kr = _rope_group(q_kr[:, Q_LORA:], tab(2), tab(3))
    kr_ref[rows, :] = kr[:, :QK_ROPE]

    q_lat = _rms(q_kr[:, :Q_LORA], g_q_ref[...]).astype(BF16)
    q_full = _dot(q_lat, w_uq_ref[...])
    q_gain = gq_ref[...]
    for h in range(N_HEADS):
        sl = slice(h * HEAD_PAD, (h + 1) * HEAD_PAD)
        qh = _rope_group(q_full[:, sl], tab(0), tab(1))
        q_ref[rows, sl] = _norm_head(qh, q_gain).astype(q_ref.dtype)

    if kv_refs:
        k_ref, v_ref = kv_refs
        cb = c.astype(BF16)
        k_full = _dot(cb, w_uk_ref[...])
        lane = lax.broadcasted_iota(jnp.int32, kr.shape, 1)
        kr_hi = jnp.where(lane >= QK_NOPE, kr, 0.0)
        k_gain = gk_ref[...]
        for h in range(N_HEADS):
            sl = slice(h * HEAD_PAD, (h + 1) * HEAD_PAD)
            k_ref[rows, sl] = _norm_head(k_full[:, sl] + kr_hi, k_gain).astype(k_ref.dtype)
        v_ref[:, rows] = _dot_nt(w_uv_ref[...], cb).astype(v_ref.dtype)

    gate = _dot(xn, w_in_ref[:, IN_G0:IN_PAD])
    gate_ref[rows, :] = jax.nn.sigmoid(gate).astype(gate_ref.dtype)


def _proj_kernel(*refs, sub):
    n_in = 11
    ins, outs = refs[:n_in], refs[n_in:]
    tm = ins[0].shape[0]
    for r in range(tm // sub):
        _proj_rows(slice(r * sub, (r + 1) * sub), *ins, *outs[:5], outs[5:])


def _run_proj(x2d, tabs, wts, gains, *, tm, with_kv, q_dtype):
    rows = x2d.shape[0]
    assert rows % tm == 0 and tabs.shape[0] % tm == 0
    tab_blocks = tabs.shape[0] // tm
    row_spec = lambda w: pl.BlockSpec((tm, w), lambda i: (i, 0))
    in_specs = [
        row_spec(D_MODEL),
        pl.BlockSpec((tm, N_TABS * LANES), lambda i: (i % tab_blocks, 0)),
        _const_spec((1, D_MODEL)), _const_spec((D_MODEL, IN_PAD)),
        _const_spec((1, Q_LORA)), _const_spec((Q_LORA, QK_PAD)),
        _const_spec((1, KV_LORA)), _const_spec((1, HEAD_PAD)),
        _const_spec((KV_LORA, QK_PAD)), _const_spec((1, HEAD_PAD)), _const_spec((N_HEADS * V_HEAD, KV_LORA)),
    ]
    out_shape = [
        jax.ShapeDtypeStruct((rows, POOL_WIDTH), F32),
        jax.ShapeDtypeStruct((rows, QK_PAD), q_dtype),
        jax.ShapeDtypeStruct((rows, KV_LORA), F32),
        jax.ShapeDtypeStruct((rows, QK_ROPE), F32),
        jax.ShapeDtypeStruct((rows, 2 * D_MODEL), BF16),
    ]
    out_specs = [row_spec(POOL_WIDTH), row_spec(QK_PAD), row_spec(KV_LORA), row_spec(QK_ROPE),
                 row_spec(2 * D_MODEL)]
    if with_kv:
        out_shape += [jax.ShapeDtypeStruct((rows, QK_PAD), BF16), jax.ShapeDtypeStruct((N_HEADS * V_HEAD, rows), BF16)]
        out_specs += [row_spec(QK_PAD), pl.BlockSpec((N_HEADS * V_HEAD, tm), lambda i: (0, i))]
    return pl.pallas_call(
        functools.partial(_proj_kernel, sub=min(tm, PROJ_SUB_ROWS)),
        grid=(rows // tm,), in_specs=in_specs, out_specs=out_specs, out_shape=out_shape,
        compiler_params=_params(1), name="proj",
    )(x2d, tabs, gains["g_attn"], wts["w_in"], gains["g_q_lat"], wts["w_uq"], gains["g_kv_lat"],
      gains["gq"], wts["w_uk"], gains["gk"], wts["w_uv_t"])


def _softmax_step(s, v, m, l, acc):
    m_new = jnp.maximum(m, jnp.max(s, axis=-1, keepdims=True))
    alpha = jnp.exp2(m - m_new)
    p = jnp.exp2(s - m_new)
    l = alpha * l + jnp.sum(p, axis=-1, keepdims=True)
    acc = alpha * acc + _dot(p.astype(BF16), v)
    return m_new, l, acc


def _softmax_step_t(s_t, v_t, m, l, acc_t):
    m_new = jnp.maximum(m, jnp.max(s_t, axis=0, keepdims=True))
    alpha = jnp.exp2(m - m_new)
    p_t = jnp.exp2(s_t - m_new)
    l = alpha * l + jnp.sum(p_t, axis=0, keepdims=True)
    acc_t = alpha * acc_t + _dot(v_t, p_t.astype(BF16))
    return m_new, l, acc_t


def _flash_kernel(q_ref, k_ref, vt_ref, km_ref, vmt_ref, o_ref, *, tq):
    causal = lax.broadcasted_iota(jnp.int32, (tq, tq), 0) <= lax.broadcasted_iota(jnp.int32, (tq, tq), 1)
    n_q = q_ref.shape[0] // tq
    block = lambda i: slice(i * tq, (i + 1) * tq)
    lanes = lambda h: slice(h * HEAD_PAD, (h + 1) * HEAD_PAD)
    tasks = [(h, i, j) for h in range(FLASH_HEADS) for i in range(n_q) for j in list(range(i, -1, -1)) + [-1]]

    def scores_t(h, i, j):
        q = q_ref[block(i), lanes(h)]
        if j < 0:
            return _dot_nt(km_ref[:, lanes(h)], q)
        s_t = _dot_nt(k_ref[block(j), lanes(h)], q)
        return jnp.where(causal, s_t, -jnp.inf) if j == i else s_t

    def values_t(h, j):
        rows = slice(h * V_HEAD, (h + 1) * V_HEAD)
        return vmt_ref[rows, :] if j < 0 else vt_ref[rows, block(j)]

    ahead = [scores_t(*t) for t in tasks[:FLASH_LOOKAHEAD]]
    for n, (h, i, j) in enumerate(tasks):
        if n + FLASH_LOOKAHEAD < len(tasks):
            ahead.append(scores_t(*tasks[n + FLASH_LOOKAHEAD]))
        s_t = ahead.pop(0)
        if j == i:
            carry = (jnp.full((1, tq), -jnp.inf, F32), jnp.zeros((1, tq), F32), jnp.zeros((V_HEAD, tq), F32))
        carry = _softmax_step_t(s_t, values_t(h, j), *carry)
        if j < 0:
            o_ref[block(i), lanes(h)] = (carry[2] / carry[1]).T.astype(o_ref.dtype)


def _run_flash(q, k, v_t, k_meta, vt_meta, *, tq):
    b, seq, _ = q.shape
    assert seq % tq == 0 and N_HEADS % FLASH_HEADS == 0
    width = FLASH_HEADS * HEAD_PAD
    head_spec = pl.BlockSpec((None, seq, width), lambda bi, h: (bi, 0, h))
    return pl.pallas_call(
        functools.partial(_flash_kernel, tq=tq),
        grid=(b, N_HEADS // FLASH_HEADS),
        in_specs=[head_spec, head_spec, pl.BlockSpec((FLASH_HEADS * V_HEAD, seq), lambda bi, h: (h, bi)),
                  pl.BlockSpec((N_META, width), lambda bi, h: (0, h)),
                  pl.BlockSpec((FLASH_HEADS * V_HEAD, N_META), lambda bi, h: (h, 0))],
        out_specs=head_spec, out_shape=jax.ShapeDtypeStruct((b, seq, N_HEADS * V_HEAD), BF16),
        compiler_params=_params(2), name="flash",
    )(q, k, v_t, k_meta, vt_meta)


def _paged_kernel(pt_ref, q_ref, cn_ref, krn_ref, w_uk_t_ref, w_uv_ref, gk_ref, cache_c_ref, cache_kr_ref,
                  o_ref, c_buf, kr_buf, sem, a_buf, qr_buf, cnew_buf, *, n_pages, n_seq, s_new, tk):
    step = pl.program_id(0)
    seq0 = step * PAGED_SEQS
    rows_hq = N_HEADS * s_new

    def page_copies(seq, sl, j):
        page = pt_ref[seq, j]
        dst = pl.ds(j * PAGE_SIZE, PAGE_SIZE)
        return (pltpu.make_async_copy(cache_c_ref.at[page], c_buf.at[sl, dst, :], sem.at[0, sl]),
                pltpu.make_async_copy(cache_kr_ref.at[page], kr_buf.at[sl, :, dst], sem.at[1, sl]))

    def start_seq(seq, sl):
        for j in range(n_pages):
            for cp in page_copies(seq, sl, j):
                cp.start()

    def wait_seq(sl):
        pltpu.make_async_copy(c_buf.at[sl], c_buf.at[sl], sem.at[0, sl]).wait()
        pltpu.make_async_copy(kr_buf.at[sl], kr_buf.at[sl], sem.at[1, sl]).wait()

    n_grp = N_HEADS // PAGED_GROUP_HEADS
    grp_nope = PAGED_GROUP_HEADS * QK_NOPE
    grp_rows = grp_nope + PAGED_GROUP_HEADS * s_new

    @pl.when(step == 0)
    def _():
        start_seq(0, 0)
        for p in range(PAGED_SEQS):
            for g in range(n_grp):
                a_buf[p, g * grp_rows:g * grp_rows + grp_nope, :] = w_uk_t_ref[g * grp_nope:(g + 1) * grp_nope, :]
        cnew_buf[...] = jnp.zeros_like(cnew_buf)

    def prepare(p):
        rows = slice(p * s_new, (p + 1) * s_new)
        qg = q_ref[rows, :] * jnp.concatenate([gk_ref[...]] * N_HEADS, axis=1)
        for h in range(N_HEADS):
            lo = h * HEAD_PAD
            q_nope = qg[:, lo:lo + QK_NOPE].astype(BF16)
            q_abs = _dot(q_nope, w_uk_t_ref[h * QK_NOPE:(h + 1) * QK_NOPE, :])
            g, hh = divmod(h, PAGED_GROUP_HEADS)
            row0 = g * grp_rows + grp_nope + hh * s_new
            a_buf[p, row0:row0 + s_new, :] = q_abs.astype(BF16)
            qr_buf[p, h * s_new:(h + 1) * s_new, :] = qg[:, lo + QK_NOPE:lo + QK_HEAD]
        cnew_buf[p, 0:s_new, :] = cn_ref[rows, :]

    shape = (rows_hq, PAGE_SIZE)
    tok = lax.broadcasted_iota(jnp.int32, shape, 0) % s_new
    new_mask = lax.broadcasted_iota(jnp.int32, shape, 1) <= tok
    n_tiles = n_pages * PAGE_SIZE // tk

    def attend(p):
        a_mats = [a_buf[p, g * grp_rows:(g + 1) * grp_rows, :] for g in range(n_grp)]
        q_rope = qr_buf[p].astype(BF16)

        def tile_matmuls(c_t, kr_t):
            c_bf = c_t.astype(BF16)
            r = [_dot_nt(a, c_bf) for a in a_mats]
            s_rope = _dot(q_rope, kr_t.astype(BF16))
            return c_bf, r, s_rope

        def tile_softmax(mats, kr_t, carry, mask):
            m, l, acc = carry
            c_bf, r, s_rope = mats
            ssq_rope = jnp.sum(kr_t * kr_t, axis=0, keepdims=True)
            parts = []
            for h in range(N_HEADS):
                g, hh = divmod(h, PAGED_GROUP_HEADS)
                kn = r[g][hh * QK_NOPE:(hh + 1) * QK_NOPE, :]
                ssq = jnp.sum(kn * kn, axis=0, keepdims=True) + ssq_rope
                rinv = lax.rsqrt(ssq * (1.0 / QK_HEAD) + EPS)
                s_nope = r[g][grp_nope + hh * s_new:grp_nope + (hh + 1) * s_new, :]
                parts.append((s_nope + s_rope[h * s_new:(h + 1) * s_new, :]) * rinv)
            s = jnp.concatenate(parts, axis=0)
            if mask is not None:
                s = jnp.where(mask, s, -jnp.inf)
            return _softmax_step(s, c_bf, m, l, acc)

        c_tile = lambda t: c_buf[p, t * tk:(t + 1) * tk, :] if t < n_tiles else cnew_buf[p]
        kr_tile = lambda t: kr_buf[p, :, t * tk:(t + 1) * tk] if t < n_tiles else krn_ref[p]
        carry = (jnp.full((rows_hq, 1), -jnp.inf, F32), jnp.zeros((rows_hq, 1), F32),
                 jnp.zeros((rows_hq, KV_LORA), F32))
        ahead = [tile_matmuls(c_tile(t), kr_tile(t)) for t in range(min(PAGED_LOOKAHEAD, n_tiles + 1))]
        for t in range(n_tiles + 1):
            if t + PAGED_LOOKAHEAD <= n_tiles:
                ahead.append(tile_matmuls(c_tile(t + PAGED_LOOKAHEAD), kr_tile(t + PAGED_LOOKAHEAD)))
            carry = tile_softmax(ahead.pop(0), kr_tile(t), carry, None if t < n_tiles else new_mask)
        m, l, acc = carry

        ctx = (acc / l).astype(BF16)
        for h in range(N_HEADS):
            o_ref[p * s_new:(p + 1) * s_new, h * V_HEAD:(h + 1) * V_HEAD] = _dot(
                ctx[h * s_new:(h + 1) * s_new, :], w_uv_ref[:, h * V_HEAD:(h + 1) * V_HEAD])

    for p in range(PAGED_SEQS):
        prepare(p)
        wait_seq(p)
        if p + 1 < PAGED_SEQS:
            start_seq(seq0 + p + 1, p + 1)
        else:
            @pl.when(seq0 + PAGED_SEQS < n_seq)
            def _():
                start_seq(seq0 + PAGED_SEQS, 0)
        attend(p)


def _run_paged(page_table, q, c_new, kr_new, wts, gains, cache_c, cache_kr, *, s_new, tk):
    n_seq, n_pages = page_table.shape
    past = n_pages * PAGE_SIZE
    assert past % tk == 0 and s_new % 8 == 0 and s_new <= PAGE_SIZE and n_seq % PAGED_SEQS == 0
    rows_hq = N_HEADS * s_new
    seq_spec = lambda w: pl.BlockSpec((PAGED_SEQS * s_new, w), lambda b, pt: (b, 0))
    const = lambda shape: pl.BlockSpec(shape, lambda b, pt: (0,) * len(shape), pipeline_mode=pl.Buffered(1))
    grid_spec = pltpu.PrefetchScalarGridSpec(
        num_scalar_prefetch=1, grid=(n_seq // PAGED_SEQS,),
        in_specs=[seq_spec(QK_PAD), seq_spec(KV_LORA),
                  pl.BlockSpec((PAGED_SEQS, QK_ROPE, PAGE_SIZE), lambda b, pt: (b, 0, 0)),
                  const((N_HEADS * QK_NOPE, KV_LORA)), const((KV_LORA, N_HEADS * V_HEAD)), const((1, HEAD_PAD)),
                  pl.BlockSpec(memory_space=pl.ANY), pl.BlockSpec(memory_space=pl.ANY)],
        out_specs=seq_spec(N_HEADS * V_HEAD),
        scratch_shapes=[
            pltpu.VMEM((PAGED_SEQS, past, KV_LORA), F32),
            pltpu.VMEM((PAGED_SEQS, QK_ROPE, past), F32),
            pltpu.SemaphoreType.DMA((2, PAGED_SEQS)),
            pltpu.VMEM((PAGED_SEQS, N_HEADS * QK_NOPE + rows_hq, KV_LORA), BF16),
            pltpu.VMEM((PAGED_SEQS, rows_hq, QK_ROPE), F32),
            pltpu.VMEM((PAGED_SEQS, PAGE_SIZE, KV_LORA), F32),
        ])
    return pl.pallas_call(
        functools.partial(_paged_kernel, n_pages=n_pages, n_seq=n_seq, s_new=s_new, tk=tk),
        grid_spec=grid_spec, out_shape=jax.ShapeDtypeStruct((n_seq * s_new, N_HEADS * V_HEAD), F32),
        compiler_params=_params(1), name="paged",
    )(page_table, q, c_new, kr_new, wts["w_uk_t"], wts["w_uv"], gains["gk"], cache_c, cache_kr)


def _pool_diff(prev, u, w):
    win, span = jnp.concatenate([prev, u], axis=0), 1
    while span < w:
        win = win + pltpu.roll(win, span, 0)
        span *= 2
    return win[N_META:, :] * (1.0 / w) - u


def _pool_sample_kernel(e_ref, d_ref, *, s_new):
    for s in range(s_new):
        t = POOL_STATE + s
        for g, w in enumerate(POOL_WINDOWS):
            cols = slice(g * POOL_GROUP_WIDTH, (g + 1) * POOL_GROUP_WIDTH)
            cur = e_ref[t, :, cols]
            win = cur
            for j in range(1, w):
                win = win + e_ref[t - j, :, cols]
            d_ref[s, :, cols] = win * (1.0 / w) - cur


def _run_pool_sample(ext_t, *, s_new):
    t, n_seq, _ = ext_t.shape
    return pl.pallas_call(
        functools.partial(_pool_sample_kernel, s_new=s_new), grid=(1,),
        in_specs=[pl.BlockSpec((t, n_seq, POOL_WIDTH), lambda i: (0, 0, 0))],
        out_specs=pl.BlockSpec((s_new, n_seq, POOL_WIDTH), lambda i: (0, 0, 0)),
        out_shape=jax.ShapeDtypeStruct((s_new, n_seq, POOL_WIDTH), F32),
        compiler_params=_params(1), name="pool_sample",
    )(ext_t)


def _mlp_kernel(*refs, ff_chunk, pool_from_u, sub):
    n_pool = 3 if pool_from_u else 1
    x_ref, att_ref, gate_ref = refs[:3]
    pool_refs = refs[3:3 + n_pool]
    w_pool_ref, pool_scale_ref, w_o_ref, g_mlp_ref, w_ff1_ref, w_ff2_ref, y_ref = refs[3 + n_pool:]
    group = lambda g: slice(g * POOL_GROUP_WIDTH, (g + 1) * POOL_GROUP_WIDTH)

    def merge(rows):
        if pool_from_u:
            u_ref, halo_ref, meta_ref = pool_refs
            if rows.start == 0:
                prev = jnp.where(pl.program_id(1) == 0, meta_ref[...], halo_ref[...])
            else:
                prev = u_ref[rows.start - N_META:rows.start, :]
            d = [_pool_diff(prev[:, group(g)], u_ref[rows, group(g)], w).astype(BF16)
                 for g, w in enumerate(POOL_WINDOWS)]
        else:
            d = [pool_refs[0][rows, group(g)].astype(BF16) for g in range(POOL_GROUPS)]
        pool = jnp.concatenate([_dot(d[g], w_pool_ref[g]) for g in range(POOL_GROUPS)], axis=1)
        g_pool = gate_ref[rows, :D_MODEL].astype(F32)
        g_att = gate_ref[rows, D_MODEL:].astype(F32)
        mix = g_pool * (pool * pool_scale_ref[...]) + g_att * att_ref[rows, :].astype(F32)
        h = x_ref[rows, :] + _dot(mix.astype(BF16), w_o_ref[...])
        return h, _rms(h, g_mlp_ref[...]).astype(BF16)

    def feed_forward(h, hn):
        acc = h
        for c0 in range(0, D_FF, ff_chunk):
            z = jnp.maximum(_dot(hn, w_ff1_ref[:, c0:c0 + ff_chunk]), 0.0)
            acc = acc + _dot((z * z).astype(BF16), w_ff2_ref[c0:c0 + ff_chunk, :])
        return acc

    tm = x_ref.shape[0]
    passes = [slice(r, r + sub) for r in range(0, tm, sub)]
    merged = merge(passes[0])
    for n, rows in enumerate(passes):
        nxt = merge(passes[n + 1]) if n + 1 < len(passes) else None
        y_ref[rows, :] = feed_forward(*merged)
        merged = nxt


def _run_mlp(x3, att3, gate3, pool_in, wts, pool_scale, g_mlp, *, tm, ff_chunk=1024):
    n_grp, rows, _ = x3.shape
    assert rows % tm == 0 and D_FF % ff_chunk == 0 and tm % N_META == 0
    row_spec = lambda w: pl.BlockSpec((None, tm, w), lambda g, i: (g, i, 0))
    pool_from_u = len(pool_in) == 2
    if pool_from_u:
        halo_blocks = tm // N_META
        pool_ops = (pool_in[0], pool_in[0], pool_in[1])
        pool_specs = [row_spec(POOL_WIDTH),
                      pl.BlockSpec((None, N_META, POOL_WIDTH),
                                   lambda g, i: (g, jnp.maximum(i * halo_blocks - 1, 0), 0)),
                      pl.BlockSpec((N_META, POOL_WIDTH), lambda g, i: (0, 0))]
    else:
        pool_ops, pool_specs = pool_in, [row_spec(POOL_WIDTH)]
    return pl.pallas_call(
        functools.partial(_mlp_kernel, ff_chunk=ff_chunk, pool_from_u=pool_from_u, sub=min(tm, MLP_SUB_ROWS)), grid=(n_grp, rows // tm),
        in_specs=[row_spec(D_MODEL), row_spec(D_MODEL), row_spec(2 * D_MODEL), *pool_specs,
                  _const_spec((POOL_GROUPS, POOL_GROUP_WIDTH, POOL_GROUP_OUT)), _const_spec((1, D_MODEL)),
                  _const_spec((D_MODEL, D_MODEL)), _const_spec((1, D_MODEL)),
                  _const_spec((D_MODEL, D_FF)), _const_spec((D_FF, D_MODEL))],
        out_specs=row_spec(D_MODEL), out_shape=jax.ShapeDtypeStruct((n_grp, rows, D_MODEL), F32),
        compiler_params=_params(2), name="mlp",
    )(x3, att3, gate3, *pool_ops, wts["w_pool"], pool_scale, wts["w_o"], g_mlp, wts["w_ff1"], wts["w_ff2"])


def _row_tile(rows, target):
    t = min(rows, target)
    while rows % t:
        t //= 2
    return t


def kernel(x_prompt, x_sample, cache_kv_latent, cache_k_rope, state_pool, page_table, meta_tokens, g_attn, w_in,
           g_q_lat, w_uq, g_kv_lat, g_qn_nope, g_qn_rope, w_uk, g_kn_nope, g_kn_rope, w_uv, w_pool_map, pool_scale,
           w_o, g_mlp, w_ff1, w_ff2):
    b, seq, _ = x_prompt.shape
    n_seq, s_new, _ = x_sample.shape
    n_pages = page_table.shape[1]
    past = n_pages * PAGE_SIZE

    wts = _prep_weights(w_in, w_uq, w_uk, w_uv, w_pool_map, w_o, w_ff1, w_ff2)
    row = lambda g: g.reshape(1, -1).astype(F32)
    gains = dict(g_attn=row(g_attn), g_q_lat=row(g_q_lat), g_kv_lat=row(g_kv_lat),
                 gq=_head_gain(g_qn_nope, g_qn_rope) * (SCALE * LOG2_E), gk=_head_gain(g_kn_nope, g_kn_rope))
    pool_scale = row(pool_scale)
    g_mlp = row(g_mlp)

    tm = _row_tile(seq, 512)
    tabs_meta = _rope_tables(np.arange(N_META))
    tabs_prompt = _rope_tables(N_META + np.arange(seq))
    tm_s = _row_tile(n_seq * s_new, 256)
    tabs_sample = _rope_tables(past + np.arange(tm_s) % s_new)

    u_m, _, c_m, kr_m, _, k_m, v_m = _run_proj(meta_tokens.astype(F32), tabs_meta, wts, gains,
                                               tm=N_META, with_kv=True, q_dtype=BF16)

    xp = x_prompt.reshape(b * seq, D_MODEL)
    u_p, q_p, c_p, kr_p, gate_p, k_p, v_p = _run_proj(xp, tabs_prompt, wts, gains,
                                                      tm=_row_tile(seq, PROJ_PROMPT_ROWS), with_kv=True,
                                                      q_dtype=BF16)
    shape3 = lambda a: a.reshape(b, seq, a.shape[-1])
    att_p = _run_flash(shape3(q_p), shape3(k_p), v_p, k_m, v_m, tq=tm)
    y_p = _run_mlp(x_prompt, att_p, shape3(gate_p), (shape3(u_p), u_m), wts, pool_scale, g_mlp,
                   tm=_row_tile(seq, MLP_PROMPT_ROWS))

    xs = x_sample.reshape(n_seq * s_new, D_MODEL)
    u_s, q_s, c_s, kr_s, gate_s = _run_proj(xs, tabs_sample, wts, gains, tm=tm_s, with_kv=False, q_dtype=F32)
    kr_new_t = jnp.pad(kr_s.reshape(n_seq, s_new, QK_ROPE).transpose(0, 2, 1),
                       ((0, 0), (0, 0), (0, PAGE_SIZE - s_new)))
    att_s = _run_paged(page_table, q_s, c_s, kr_new_t, wts, gains, cache_kv_latent,
                       cache_k_rope.transpose(0, 2, 1), s_new=s_new, tk=_row_tile(past, PAGED_KEY_TILE))
    ext_s = jnp.concatenate([state_pool.astype(F32), u_s.reshape(n_seq, s_new, POOL_WIDTH)], axis=1)
    d_s = _run_pool_sample(ext_s.transpose(1, 0, 2), s_new=s_new).transpose(1, 0, 2)
    y_s = _run_mlp(xs[None], att_s[None], gate_s[None], (d_s.reshape(1, n_seq * s_new, POOL_WIDTH),), wts,
                   pool_scale, g_mlp, tm=tm_s)

    rep = lambda a: jnp.broadcast_to(a[None], (b,) + a.shape)
    return (
        y_p,
        y_s.reshape(n_seq, s_new, D_MODEL),
        jnp.concatenate([rep(c_m), shape3(c_p)], axis=1),
        jnp.concatenate([rep(kr_m), shape3(kr_p)], axis=1),
        shape3(u_p)[:, seq - POOL_STATE:],
        c_s.reshape(n_seq, s_new, KV_LORA),
        kr_s.reshape(n_seq, s_new, QK_ROPE),
        ext_s[:, s_new:],
    )
```

```python
import functools

import numpy as np
import jax
import jax.numpy as jnp
from jax import lax
from jax.experimental import pallas as pl
from jax.experimental.pallas import tpu as pltpu

D_MODEL = 1024
N_META = 16
N_HEADS = 8
QK_NOPE = 64
QK_ROPE = 32
ROPE_HALF = QK_ROPE // 2
QK_HEAD = QK_NOPE + QK_ROPE
V_HEAD = D_MODEL // N_HEADS
Q_LORA = 3 * D_MODEL // 8
KV_LORA = D_MODEL // 4
POOL_WIDTH = D_MODEL // 2
POOL_WINDOWS = (2, 4, 8, 16)
POOL_GROUPS = len(POOL_WINDOWS)
POOL_GROUP_WIDTH = POOL_WIDTH // POOL_GROUPS
POOL_GROUP_OUT = D_MODEL // POOL_GROUPS
POOL_STATE = max(POOL_WINDOWS) - 1
D_FF = 4 * D_MODEL
ROPE_BASE = 10000.0
EPS = 1e-6
SCALE = QK_HEAD ** -0.5
LOG2_E = 1.4426950408889634
PAGE_SIZE = 128

LANES = 128
V7X_VMEM_LIMIT_BYTES = 56 * 2 ** 20

HEAD_PAD = LANES
QK_PAD = N_HEADS * HEAD_PAD
KR_PAD = LANES
RAW_Q0, RAW_C0, RAW_KR0 = POOL_WIDTH, POOL_WIDTH + Q_LORA, POOL_WIDTH + Q_LORA + KV_LORA
RAW_G0 = RAW_KR0 + QK_ROPE
IN_U0, IN_Q0, IN_KR0 = 0, POOL_WIDTH, POOL_WIDTH + Q_LORA
IN_C0 = IN_KR0 + KR_PAD
IN_G0 = IN_C0 + KV_LORA
IN_PAD = IN_G0 + 2 * D_MODEL
N_TABS = 4
PROJ_PROMPT_ROWS = 1024
PROJ_SUB_ROWS = 512
MLP_PROMPT_ROWS = 1024
MLP_SUB_ROWS = 256
PAGED_SEQS = 2
PAGED_GROUP_HEADS = 8
PAGED_KEY_TILE = 2048
FLASH_HEADS = 4
FLASH_LOOKAHEAD = 1
PAGED_LOOKAHEAD = 2

F32 = jnp.float32
BF16 = jnp.bfloat16


def _dot(a, b):
    return jnp.dot(a, b, preferred_element_type=F32)


def _dot_nt(a, b):
    return lax.dot_general(a, b, (((1,), (1,)), ((), ())), preferred_element_type=F32)


def _rms(x, g):
    return x * lax.rsqrt(jnp.mean(x * x, axis=-1, keepdims=True) + EPS) * g


def _const_spec(shape):
    zeros = (0,) * len(shape)
    return pl.BlockSpec(shape, lambda *_: zeros, pipeline_mode=pl.Buffered(1))


def _params(n_axes):
    return pltpu.CompilerParams(dimension_semantics=("arbitrary",) * n_axes,
                                vmem_limit_bytes=V7X_VMEM_LIMIT_BYTES)


def _rope_tables(pos):
    pos = np.asarray(pos)
    p = pos.shape[0]
    f32 = np.float32
    inv = ROPE_BASE ** (-np.arange(0, QK_ROPE, 2, dtype=np.float64) / QK_ROPE)
    ang = pos.astype(np.float64)[:, None] * inv[None, :]
    cos, sin = np.cos(ang).astype(f32), np.sin(ang).astype(f32)
    z = lambda n: np.zeros((p, n), f32)
    q_cos = np.concatenate([np.ones((p, QK_NOPE), f32), cos, cos, z(HEAD_PAD - QK_HEAD)], axis=1)
    q_sin = np.concatenate([z(QK_NOPE), -sin, sin, z(HEAD_PAD - QK_HEAD)], axis=1)
    half_cos = np.concatenate([cos, cos, z(KR_PAD // 2 - QK_ROPE)], axis=1)
    half_sin = np.concatenate([-sin, sin, z(KR_PAD // 2 - QK_ROPE)], axis=1)
    return jnp.asarray(np.concatenate([q_cos, q_sin, half_cos, half_cos, half_sin, half_sin], axis=1))


def _head_gain(g_nope, g_rope):
    g = jnp.concatenate([g_nope, g_rope, g_rope, jnp.zeros((HEAD_PAD - QK_HEAD,), F32)])
    return g.reshape(1, HEAD_PAD).astype(F32)


def _prep_weights(w_in, w_uq, w_uk, w_uv, w_pool_map, w_o, w_ff1, w_ff2):
    w_kr = w_in[:, RAW_KR0:RAW_G0]
    kr_half = jnp.concatenate([w_kr, w_kr[:, :ROPE_HALF], jnp.zeros((D_MODEL, ROPE_HALF), w_in.dtype)], axis=1)
    w_in_pad = jnp.concatenate([w_in[:, :RAW_C0], kr_half, kr_half, w_in[:, RAW_C0:RAW_KR0], w_in[:, RAW_G0:]],
                               axis=1).astype(BF16)
    uq = w_uq.reshape(Q_LORA, N_HEADS, QK_HEAD)
    uq = jnp.concatenate([uq, uq[:, :, QK_NOPE:QK_NOPE + ROPE_HALF],
                          jnp.zeros((Q_LORA, N_HEADS, HEAD_PAD - QK_HEAD - ROPE_HALF), uq.dtype)], axis=2)
    uq = uq.reshape(Q_LORA, QK_PAD).astype(BF16)
    uk = jnp.pad(w_uk, ((0, 0), (0, 0), (0, HEAD_PAD - QK_NOPE))).reshape(KV_LORA, QK_PAD).astype(BF16)
    uk_t = w_uk.reshape(KV_LORA, N_HEADS * QK_NOPE).T.astype(BF16)
    uv = w_uv.reshape(KV_LORA, N_HEADS * V_HEAD).astype(BF16)
    return dict(w_in=w_in_pad, w_uq=uq, w_uk=uk, w_uk_t=uk_t, w_uv=uv, w_uv_t=uv.T, w_pool=w_pool_map.astype(BF16),
                w_o=w_o.astype(BF16), w_ff1=w_ff1.astype(BF16), w_ff2=w_ff2.astype(BF16))


def _rope_group(x, cos, sin):
    return x * cos + pltpu.roll(x, LANES - ROPE_HALF, 1) * sin


def _norm_head(x, gain):
    ssq = jnp.sum(x * x, axis=-1, keepdims=True)
    return x * lax.rsqrt(ssq * (1.0 / QK_HEAD) + EPS) * gain


def _proj_rows(rows, x_ref, tab_ref, g_attn_ref, w_in_ref, g_q_ref, w_uq_ref, g_kv_ref, gq_ref,
               w_uk_ref, gk_ref, w_uv_ref, u_ref, q_ref, c_ref, kr_ref, gate_ref, kv_refs):
    xn = _rms(x_ref[rows, :], g_attn_ref[...]).astype(BF16)
    u_ref[rows, :] = _dot(xn, w_in_ref[:, IN_U0:IN_Q0])

    c = _rms(_dot(xn, w_in_ref[:, IN_C0:IN_G0]), g_kv_ref[...])
    c_ref[rows, :] = c

    tab = lambda t: tab_ref[rows, t * LANES:(t + 1) * LANES]
    q_kr = _dot(xn, w_in_ref[:, IN_Q0:IN_C0])
    kr = _rope_group(q_kr[:, Q_LORA:], tab(2), tab(3))
    kr_ref[rows, :] = kr[:, :QK_ROPE]

    q_lat = _rms(q_kr[:, :Q_LORA], g_q_ref[...]).astype(BF16)
    q_full = _dot(q_lat, w_uq_ref[...])
    q_gain = gq_ref[...]
    for h in range(N_HEADS):
        sl = slice(h * HEAD_PAD, (h + 1) * HEAD_PAD)
        qh = _rope_group(q_full[:, sl], tab(0), tab(1))
        q_ref[rows, sl] = _norm_head(qh, q_gain).astype(q_ref.dtype)

    if kv_refs:
        k_ref, v_ref = kv_refs
        cb = c.astype(BF16)
        k_full = _dot(cb, w_uk_ref[...])
        lane = lax.broadcasted_iota(jnp.int32, kr.shape, 1)
        kr_hi = jnp.where(lane >= QK_NOPE, kr, 0.0)
        k_gain = gk_ref[...]
        for h in range(N_HEADS):
            sl = slice(h * HEAD_PAD, (h + 1) * HEAD_PAD)
            k_ref[rows, sl] = _norm_head(k_full[:, sl] + kr_hi, k_gain).astype(k_ref.dtype)
        v_ref[:, rows] = _dot_nt(w_uv_ref[...], cb).astype(v_ref.dtype)

    gate = _dot(xn, w_in_ref[:, IN_G0:IN_PAD])
    gate_ref[rows, :] = jax.nn.sigmoid(gate).astype(gate_ref.dtype)


def _proj_kernel(*refs, sub):
    n_in = 11
    ins, outs = refs[:n_in], refs[n_in:]
    tm = ins[0].shape[0]
    for r in range(tm // sub):
        _proj_rows(slice(r * sub, (r + 1) * sub), *ins, *outs[:5], outs[5:])


def _run_proj(x2d, tabs, wts, gains, *, tm, with_kv, q_dtype):
    rows = x2d.shape[0]
    assert rows % tm == 0 and tabs.shape[0] % tm == 0
    tab_blocks = tabs.shape[0] // tm
    row_spec = lambda w: pl.BlockSpec((tm, w), lambda i: (i, 0))
    in_specs = [
        row_spec(D_MODEL),
        pl.BlockSpec((tm, N_TABS * LANES), lambda i: (i % tab_blocks, 0)),
        _const_spec((1, D_MODEL)), _const_spec((D_MODEL, IN_PAD)),
        _const_spec((1, Q_LORA)), _const_spec((Q_LORA, QK_PAD)),
        _const_spec((1, KV_LORA)), _const_spec((1, HEAD_PAD)),
        _const_spec((KV_LORA, QK_PAD)), _const_spec((1, HEAD_PAD)), _const_spec((N_HEADS * V_HEAD, KV_LORA)),
    ]
    out_shape = [
        jax.ShapeDtypeStruct((rows, POOL_WIDTH), F32),
        jax.ShapeDtypeStruct((rows, QK_PAD), q_dtype),
        jax.ShapeDtypeStruct((rows, KV_LORA), F32),
        jax.ShapeDtypeStruct((rows, QK_ROPE), F32),
        jax.ShapeDtypeStruct((rows, 2 * D_MODEL), BF16),
    ]
    out_specs = [row_spec(POOL_WIDTH), row_spec(QK_PAD), row_spec(KV_LORA), row_spec(QK_ROPE),
                 row_spec(2 * D_MODEL)]
    if with_kv:
        out_shape += [jax.ShapeDtypeStruct((rows, QK_PAD), BF16), jax.ShapeDtypeStruct((N_HEADS * V_HEAD, rows), BF16)]
        out_specs += [row_spec(QK_PAD), pl.BlockSpec((N_HEADS * V_HEAD, tm), lambda i: (0, i))]
    return pl.pallas_call(
        functools.partial(_proj_kernel, sub=min(tm, PROJ_SUB_ROWS)),
        grid=(rows // tm,), in_specs=in_specs, out_specs=out_specs, out_shape=out_shape,
        compiler_params=_params(1), name="proj",
    )(x2d, tabs, gains["g_attn"], wts["w_in"], gains["g_q_lat"], wts["w_uq"], gains["g_kv_lat"],
      gains["gq"], wts["w_uk"], gains["gk"], wts["w_uv_t"])


def _softmax_step(s, v, m, l, acc):
    m_new = jnp.maximum(m, jnp.max(s, axis=-1, keepdims=True))
    alpha = jnp.exp2(m - m_new)
    p = jnp.exp2(s - m_new)
    l = alpha * l + jnp.sum(p, axis=-1, keepdims=True)
    acc = alpha * acc + _dot(p.astype(BF16), v)
    return m_new, l, acc


def _softmax_step_t(s_t, v_t, m, l, acc_t):
    m_new = jnp.maximum(m, jnp.max(s_t, axis=0, keepdims=True))
    alpha = jnp.exp2(m - m_new)
    p_t = jnp.exp2(s_t - m_new)
    l = alpha * l + jnp.sum(p_t, axis=0, keepdims=True)
    acc_t = alpha * acc_t + _dot(v_t, p_t.astype(BF16))
    return m_new, l, acc_t


def _flash_kernel(q_ref, k_ref, vt_ref, km_ref, vmt_ref, o_ref, *, tq):
    causal = lax.broadcasted_iota(jnp.int32, (tq, tq), 0) <= lax.broadcasted_iota(jnp.int32, (tq, tq), 1)
    n_q = q_ref.shape[0] // tq
    block = lambda i: slice(i * tq, (i + 1) * tq)
    lanes = lambda h: slice(h * HEAD_PAD, (h + 1) * HEAD_PAD)
    tasks = [(h, i, j) for h in range(FLASH_HEADS) for i in range(n_q) for j in list(range(i, -1, -1)) + [-1]]

    def scores_t(h, i, j):
        q = q_ref[block(i), lanes(h)]
        if j < 0:
            return _dot_nt(km_ref[:, lanes(h)], q)
        s_t = _dot_nt(k_ref[block(j), lanes(h)], q)
        return jnp.where(causal, s_t, -jnp.inf) if j == i else s_t

    def values_t(h, j):
        rows = slice(h * V_HEAD, (h + 1) * V_HEAD)
        return vmt_ref[rows, :] if j < 0 else vt_ref[rows, block(j)]

    ahead = [scores_t(*t) for t in tasks[:FLASH_LOOKAHEAD]]
    for n, (h, i, j) in enumerate(tasks):
        if n + FLASH_LOOKAHEAD < len(tasks):
            ahead.append(scores_t(*tasks[n + FLASH_LOOKAHEAD]))
        s_t = ahead.pop(0)
        if j == i:
            carry = (jnp.full((1, tq), -jnp.inf, F32), jnp.zeros((1, tq), F32), jnp.zeros((V_HEAD, tq), F32))
        carry = _softmax_step_t(s_t, values_t(h, j), *carry)
        if j < 0:
            o_ref[block(i), lanes(h)] = (carry[2] / carry[1]).T.astype(o_ref.dtype)


def _run_flash(q, k, v_t, k_meta, vt_meta, *, tq):
    b, seq, _ = q.shape
    assert seq % tq == 0 and N_HEADS % FLASH_HEADS == 0
    width = FLASH_HEADS * HEAD_PAD
    head_spec = pl.BlockSpec((None, seq, width), lambda bi, h: (bi, 0, h))
    return pl.pallas_call(
        functools.partial(_flash_kernel, tq=tq),
        grid=(b, N_HEADS // FLASH_HEADS),
        in_specs=[head_spec, head_spec, pl.BlockSpec((FLASH_HEADS * V_HEAD, seq), lambda bi, h: (h, bi)),
                  pl.BlockSpec((N_META, width), lambda bi, h: (0, h)),
                  pl.BlockSpec((FLASH_HEADS * V_HEAD, N_META), lambda bi, h: (h, 0))],
        out_specs=head_spec, out_shape=jax.ShapeDtypeStruct((b, seq, N_HEADS * V_HEAD), BF16),
        compiler_params=_params(2), name="flash",
    )(q, k, v_t, k_meta, vt_meta)


def _paged_kernel(pt_ref, q_ref, cn_ref, krn_ref, w_uk_t_ref, w_uv_ref, gk_ref, cache_c_ref, cache_kr_ref,
                  o_ref, c_buf, kr_buf, sem, a_buf, qr_buf, cnew_buf, *, n_pages, n_seq, s_new, tk):
    step = pl.program_id(0)
    seq0 = step * PAGED_SEQS
    rows_hq = N_HEADS * s_new

    def page_copies(seq, sl, j):
        page = pt_ref[seq, j]
        dst = pl.ds(j * PAGE_SIZE, PAGE_SIZE)
        return (pltpu.make_async_copy(cache_c_ref.at[page], c_buf.at[sl, dst, :], sem.at[0, sl]),
                pltpu.make_async_copy(cache_kr_ref.at[page], kr_buf.at[sl, :, dst], sem.at[1, sl]))

    def start_seq(seq, sl):
        for j in range(n_pages):
            for cp in page_copies(seq, sl, j):
                cp.start(priority=j % 2)

    def wait_seq(sl):
        pltpu.make_async_copy(c_buf.at[sl], c_buf.at[sl], sem.at[0, sl]).wait()
        pltpu.make_async_copy(kr_buf.at[sl], kr_buf.at[sl], sem.at[1, sl]).wait()

    n_grp = N_HEADS // PAGED_GROUP_HEADS
    grp_nope = PAGED_GROUP_HEADS * QK_NOPE
    grp_rows = grp_nope + PAGED_GROUP_HEADS * s_new

    @pl.when(step == 0)
    def _():
        start_seq(0, 0)
        for p in range(PAGED_SEQS):
            for g in range(n_grp):
                a_buf[p, g * grp_rows:g * grp_rows + grp_nope, :] = w_uk_t_ref[g * grp_nope:(g + 1) * grp_nope, :]
        cnew_buf[...] = jnp.zeros_like(cnew_buf)

    def prepare(p):
        rows = slice(p * s_new, (p + 1) * s_new)
        qg = q_ref[rows, :] * jnp.concatenate([gk_ref[...]] * N_HEADS, axis=1)
        for h in range(N_HEADS):
            lo = h * HEAD_PAD
            q_nope = qg[:, lo:lo + QK_NOPE].astype(BF16)
            q_abs = _dot(q_nope, w_uk_t_ref[h * QK_NOPE:(h + 1) * QK_NOPE, :])
            g, hh = divmod(h, PAGED_GROUP_HEADS)
            row0 = g * grp_rows + grp_nope + hh * s_new
            a_buf[p, row0:row0 + s_new, :] = q_abs.astype(BF16)
            qr_buf[p, h * s_new:(h + 1) * s_new, :] = qg[:, lo + QK_NOPE:lo + QK_HEAD]
        cnew_buf[p, 0:s_new, :] = cn_ref[rows, :]

    shape = (rows_hq, PAGE_SIZE)
    tok = lax.broadcasted_iota(jnp.int32, shape, 0) % s_new
    new_mask = lax.broadcasted_iota(jnp.int32, shape, 1) <= tok
    n_tiles = n_pages * PAGE_SIZE // tk

    def attend(p):
        a_mats = [a_buf[p, g * grp_rows:(g + 1) * grp_rows, :] for g in range(n_grp)]
        q_rope = qr_buf[p].astype(BF16)

        def tile_matmuls(c_t, kr_t):
            c_bf = c_t.astype(BF16)
            r = [_dot_nt(a, c_bf) for a in a_mats]
            s_rope = _dot(q_rope, kr_t.astype(BF16))
            return c_bf, r, s_rope

        def tile_softmax(mats, kr_t, carry, mask):
            m, l, acc = carry
            c_bf, r, s_rope = mats
            ssq_rope = jnp.sum(kr_t * kr_t, axis=0, keepdims=True)
            parts = []
            for h in range(N_HEADS):
                g, hh = divmod(h, PAGED_GROUP_HEADS)
                kn = r[g][hh * QK_NOPE:(hh + 1) * QK_NOPE, :]
                ssq = jnp.sum(kn * kn, axis=0, keepdims=True) + ssq_rope
                rinv = lax.rsqrt(ssq * (1.0 / QK_HEAD) + EPS)
                s_nope = r[g][grp_nope + hh * s_new:grp_nope + (hh + 1) * s_new, :]
                parts.append((s_nope + s_rope[h * s_new:(h + 1) * s_new, :]) * rinv)
            s = jnp.concatenate(parts, axis=0)
            if mask is not None:
                s = jnp.where(mask, s, -jnp.inf)
            return _softmax_step(s, c_bf, m, l, acc)

        c_tile = lambda t: c_buf[p, t * tk:(t + 1) * tk, :] if t < n_tiles else cnew_buf[p]
        kr_tile = lambda t: kr_buf[p, :, t * tk:(t + 1) * tk] if t < n_tiles else krn_ref[p]
        carry = (jnp.full((rows_hq, 1), -jnp.inf, F32), jnp.zeros((rows_hq, 1), F32),
                 jnp.zeros((rows_hq, KV_LORA), F32))
        ahead = [tile_matmuls(c_tile(t), kr_tile(t)) for t in range(min(PAGED_LOOKAHEAD, n_tiles + 1))]
        for t in range(n_tiles + 1):
            if t + PAGED_LOOKAHEAD <= n_tiles:
                ahead.append(tile_matmuls(c_tile(t + PAGED_LOOKAHEAD), kr_tile(t + PAGED_LOOKAHEAD)))
            carry = tile_softmax(ahead.pop(0), kr_tile(t), carry, None if t < n_tiles else new_mask)
        m, l, acc = carry

        ctx = (acc / l).astype(BF16)
        for h in range(N_HEADS):
            o_ref[p * s_new:(p + 1) * s_new, h * V_HEAD:(h + 1) * V_HEAD] = _dot(
                ctx[h * s_new:(h + 1) * s_new, :], w_uv_ref[:, h * V_HEAD:(h + 1) * V_HEAD])

    for p in range(PAGED_SEQS):
        prepare(p)
        wait_seq(p)
        if p + 1 < PAGED_SEQS:
            start_seq(seq0 + p + 1, p + 1)
        else:
            @pl.when(seq0 + PAGED_SEQS < n_seq)
            def _():
                start_seq(seq0 + PAGED_SEQS, 0)
        attend(p)


def _run_paged(page_table, q, c_new, kr_new, wts, gains, cache_c, cache_kr, *, s_new, tk):
    n_seq, n_pages = page_table.shape
    past = n_pages * PAGE_SIZE
    assert past % tk == 0 and s_new % 8 == 0 and s_new <= PAGE_SIZE and n_seq % PAGED_SEQS == 0
    rows_hq = N_HEADS * s_new
    seq_spec = lambda w: pl.BlockSpec((PAGED_SEQS * s_new, w), lambda b, pt: (b, 0))
    const = lambda shape: pl.BlockSpec(shape, lambda b, pt: (0,) * len(shape), pipeline_mode=pl.Buffered(1))
    grid_spec = pltpu.PrefetchScalarGridSpec(
        num_scalar_prefetch=1, grid=(n_seq // PAGED_SEQS,),
        in_specs=[seq_spec(QK_PAD), seq_spec(KV_LORA),
                  pl.BlockSpec((PAGED_SEQS, QK_ROPE, PAGE_SIZE), lambda b, pt: (b, 0, 0)),
                  const((N_HEADS * QK_NOPE, KV_LORA)), const((KV_LORA, N_HEADS * V_HEAD)), const((1, HEAD_PAD)),
                  pl.BlockSpec(memory_space=pl.ANY), pl.BlockSpec(memory_space=pl.ANY)],
        out_specs=seq_spec(N_HEADS * V_HEAD),
        scratch_shapes=[
            pltpu.VMEM((PAGED_SEQS, past, KV_LORA), F32),
            pltpu.VMEM((PAGED_SEQS, QK_ROPE, past), F32),
            pltpu.SemaphoreType.DMA((2, PAGED_SEQS)),
            pltpu.VMEM((PAGED_SEQS, N_HEADS * QK_NOPE + rows_hq, KV_LORA), BF16),
            pltpu.VMEM((PAGED_SEQS, rows_hq, QK_ROPE), F32),
            pltpu.VMEM((PAGED_SEQS, PAGE_SIZE, KV_LORA), F32),
        ])
    return pl.pallas_call(
        functools.partial(_paged_kernel, n_pages=n_pages, n_seq=n_seq, s_new=s_new, tk=tk),
        grid_spec=grid_spec, out_shape=jax.ShapeDtypeStruct((n_seq * s_new, N_HEADS * V_HEAD), F32),
        compiler_params=_params(1), name="paged",
    )(page_table, q, c_new, kr_new, wts["w_uk_t"], wts["w_uv"], gains["gk"], cache_c, cache_kr)


def _pool_diff(prev, u, w):
    win, span = jnp.concatenate([prev, u], axis=0), 1
    while span < w:
        win = win + pltpu.roll(win, span, 0)
        span *= 2
    return win[N_META:, :] * (1.0 / w) - u


def _pool_sample_kernel(e_ref, d_ref, *, s_new):
    for s in range(s_new):
        t = POOL_STATE + s
        for g, w in enumerate(POOL_WINDOWS):
            cols = slice(g * POOL_GROUP_WIDTH, (g + 1) * POOL_GROUP_WIDTH)
            cur = e_ref[t, :, cols]
            win = cur
            for j in range(1, w):
                win = win + e_ref[t - j, :, cols]
            d_ref[s, :, cols] = win * (1.0 / w) - cur


def _run_pool_sample(ext_t, *, s_new):
    t, n_seq, _ = ext_t.shape
    return pl.pallas_call(
        functools.partial(_pool_sample_kernel, s_new=s_new), grid=(1,),
        in_specs=[pl.BlockSpec((t, n_seq, POOL_WIDTH), lambda i: (0, 0, 0))],
        out_specs=pl.BlockSpec((s_new, n_seq, POOL_WIDTH), lambda i: (0, 0, 0)),
        out_shape=jax.ShapeDtypeStruct((s_new, n_seq, POOL_WIDTH), F32),
        compiler_params=_params(1), name="pool_sample",
    )(ext_t)


def _mlp_kernel(*refs, ff_chunk, pool_from_u, sub):
    n_pool = 3 if pool_from_u else 1
    x_ref, att_ref, gate_ref = refs[:3]
    pool_refs = refs[3:3 + n_pool]
    w_pool_ref, pool_scale_ref, w_o_ref, g_mlp_ref, w_ff1_ref, w_ff2_ref, y_ref = refs[3 + n_pool:]
    group = lambda g: slice(g * POOL_GROUP_WIDTH, (g + 1) * POOL_GROUP_WIDTH)

    def merge(rows):
        if pool_from_u:
            u_ref, halo_ref, meta_ref = pool_refs
            if rows.start == 0:
                prev = jnp.where(pl.program_id(1) == 0, meta_ref[...], halo_ref[...])
            else:
                prev = u_ref[rows.start - N_META:rows.start, :]
            d = [_pool_diff(prev[:, group(g)], u_ref[rows, group(g)], w).astype(BF16)
                 for g, w in enumerate(POOL_WINDOWS)]
        else:
            d = [pool_refs[0][rows, group(g)].astype(BF16) for g in range(POOL_GROUPS)]
        pool = jnp.concatenate([_dot(d[g], w_pool_ref[g]) for g in range(POOL_GROUPS)], axis=1)
        g_pool = gate_ref[rows, :D_MODEL].astype(F32)
        g_att = gate_ref[rows, D_MODEL:].astype(F32)
        mix = g_pool * (pool * pool_scale_ref[...]) + g_att * att_ref[rows, :].astype(F32)
        h = x_ref[rows, :] + _dot(mix.astype(BF16), w_o_ref[...])
        return h, _rms(h, g_mlp_ref[...]).astype(BF16)

    def feed_forward(h, hn):
        acc = h
        for c0 in range(0, D_FF, ff_chunk):
            z = jnp.maximum(_dot(hn, w_ff1_ref[:, c0:c0 + ff_chunk]), 0.0)
            acc = acc + _dot((z * z).astype(BF16), w_ff2_ref[c0:c0 + ff_chunk, :])
        return acc

    tm = x_ref.shape[0]
    passes = [slice(r, r + sub) for r in range(0, tm, sub)]
    merged = merge(passes[0])
    for n, rows in enumerate(passes):
        nxt = merge(passes[n + 1]) if n + 1 < len(passes) else None
        y_ref[rows, :] = feed_forward(*merged)
        merged = nxt


def _run_mlp(x3, att3, gate3, pool_in, wts, pool_scale, g_mlp, *, tm, ff_chunk=1024):
    n_grp, rows, _ = x3.shape
    assert rows % tm == 0 and D_FF % ff_chunk == 0 and tm % N_META == 0
    row_spec = lambda w: pl.BlockSpec((None, tm, w), lambda g, i: (g, i, 0))
    pool_from_u = len(pool_in) == 2
    if pool_from_u:
        halo_blocks = tm // N_META
        pool_ops = (pool_in[0], pool_in[0], pool_in[1])
        pool_specs = [row_spec(POOL_WIDTH),
                      pl.BlockSpec((None, N_META, POOL_WIDTH),
                                   lambda g, i: (g, jnp.maximum(i * halo_blocks - 1, 0), 0)),
                      pl.BlockSpec((N_META, POOL_WIDTH), lambda g, i: (0, 0))]
    else:
        pool_ops, pool_specs = pool_in, [row_spec(POOL_WIDTH)]
    return pl.pallas_call(
        functools.partial(_mlp_kernel, ff_chunk=ff_chunk, pool_from_u=pool_from_u, sub=min(tm, MLP_SUB_ROWS)), grid=(n_grp, rows // tm),
        in_specs=[row_spec(D_MODEL), row_spec(D_MODEL), row_spec(2 * D_MODEL), *pool_specs,
                  _const_spec((POOL_GROUPS, POOL_GROUP_WIDTH, POOL_GROUP_OUT)), _const_spec((1, D_MODEL)),
                  _const_spec((D_MODEL, D_MODEL)), _const_spec((1, D_MODEL)),
                  _const_spec((D_MODEL, D_FF)), _const_spec((D_FF, D_MODEL))],
        out_specs=row_spec(D_MODEL), out_shape=jax.ShapeDtypeStruct((n_grp, rows, D_MODEL), F32),
        compiler_params=_params(2), name="mlp",
    )(x3, att3, gate3, *pool_ops, wts["w_pool"], pool_scale, wts["w_o"], g_mlp, wts["w_ff1"], wts["w_ff2"])


def _row_tile(rows, target):
    t = min(rows, target)
    while rows % t:
        t //= 2
    return t


def kernel(x_prompt, x_sample, cache_kv_latent, cache_k_rope, state_pool, page_table, meta_tokens, g_attn, w_in,
           g_q_lat, w_uq, g_kv_lat, g_qn_nope, g_qn_rope, w_uk, g_kn_nope, g_kn_rope, w_uv, w_pool_map, pool_scale,
           w_o, g_mlp, w_ff1, w_ff2):
    b, seq, _ = x_prompt.shape
    n_seq, s_new, _ = x_sample.shape
    n_pages = page_table.shape[1]
    past = n_pages * PAGE_SIZE

    wts = _prep_weights(w_in, w_uq, w_uk, w_uv, w_pool_map, w_o, w_ff1, w_ff2)
    row = lambda g: g.reshape(1, -1).astype(F32)
    gains = dict(g_attn=row(g_attn), g_q_lat=row(g_q_lat), g_kv_lat=row(g_kv_lat),
                 gq=_head_gain(g_qn_nope, g_qn_rope) * (SCALE * LOG2_E), gk=_head_gain(g_kn_nope, g_kn_rope))
    pool_scale = row(pool_scale)
    g_mlp = row(g_mlp)

    tm = _row_tile(seq, 512)
    tabs_meta = _rope_tables(np.arange(N_META))
    tabs_prompt = _rope_tables(N_META + np.arange(seq))
    tm_s = _row_tile(n_seq * s_new, 256)
    tabs_sample = _rope_tables(past + np.arange(tm_s) % s_new)

    u_m, _, c_m, kr_m, _, k_m, v_m = _run_proj(meta_tokens.astype(F32), tabs_meta, wts, gains,
                                               tm=N_META, with_kv=True, q_dtype=BF16)

    xp = x_prompt.reshape(b * seq, D_MODEL)
    u_p, q_p, c_p, kr_p, gate_p, k_p, v_p = _run_proj(xp, tabs_prompt, wts, gains,
                                                      tm=_row_tile(seq, PROJ_PROMPT_ROWS), with_kv=True,
                                                      q_dtype=BF16)
    shape3 = lambda a: a.reshape(b, seq, a.shape[-1])
    att_p = _run_flash(shape3(q_p), shape3(k_p), v_p, k_m, v_m, tq=tm)
    y_p = _run_mlp(x_prompt, att_p, shape3(gate_p), (shape3(u_p), u_m), wts, pool_scale, g_mlp,
                   tm=_row_tile(seq, MLP_PROMPT_ROWS))

    xs = x_sample.reshape(n_seq * s_new, D_MODEL)
    u_s, q_s, c_s, kr_s, gate_s = _run_proj(xs, tabs_sample, wts, gains, tm=tm_s, with_kv=False, q_dtype=F32)
    kr_new_t = jnp.pad(kr_s.reshape(n_seq, s_new, QK_ROPE).transpose(0, 2, 1),
                       ((0, 0), (0, 0), (0, PAGE_SIZE - s_new)))
    att_s = _run_paged(page_table, q_s, c_s, kr_new_t, wts, gains, cache_kv_latent,
                       cache_k_rope.transpose(0, 2, 1), s_new=s_new, tk=_row_tile(past, PAGED_KEY_TILE))
    ext_s = jnp.concatenate([state_pool.astype(F32), u_s.reshape(n_seq, s_new, POOL_WIDTH)], axis=1)
    d_s = _run_pool_sample(ext_s.transpose(1, 0, 2), s_new=s_new).transpose(1, 0, 2)
    y_s = _run_mlp(xs[None], att_s[None], gate_s[None], (d_s.reshape(1, n_seq * s_new, POOL_WIDTH),), wts,
                   pool_scale, g_mlp, tm=tm_s)

    rep = lambda a: jnp.broadcast_to(a[None], (b,) + a.shape)
    return (
        y_p,
        y_s.reshape(n_seq, s_new, D_MODEL),
        jnp.concatenate([rep(c_m), shape3(c_p)], axis=1),
        jnp.concatenate([rep(kr_m), shape3(kr_p)], axis=1),
        shape3(u_p)[:, seq - POOL_STATE:],
        c_s.reshape(n_seq, s_new, KV_LORA),
        kr_s.reshape(n_seq, s_new, QK_ROPE),
        ext_s[:, s_new:],
    )
```
